```python
import math
import jax
import jax.numpy as jnp
from jax import lax
import numpy as np

D_MODEL = 2048
BATCH = 16
SEQ = 2048
DEPTH = 2

GRID_W = 64
CTX_LEN = 256
EPS = 1e-6
N_EVEN = (DEPTH + 1) // 2
N_ODD = DEPTH // 2

CONV_WIDTH = D_MODEL // 2
CONV_K = 3
LRU_WIDTH = D_MODEL // 2
LRU_HEADS = 8
LRU_BLOCK = LRU_WIDTH // LRU_HEADS
LRU_CONV_K = 4
LRU_C = 8.0
EVEN_IN_COLS = 3 * CONV_WIDTH + 2 * LRU_WIDTH
DIFF_HEADS = 8
DIFF_QK_DIM = 128
DIFF_V_DIM = 2 * DIFF_QK_DIM
DIFF_QK_COLS = DIFF_HEADS * 2 * DIFF_QK_DIM
DIFF_V_COLS = DIFF_HEADS * DIFF_V_DIM
ODD_IN_COLS = 2 * DIFF_QK_COLS + DIFF_V_COLS
ROPE_AXIS_DIM = DIFF_QK_DIM // 2
ROPE_BASE = 10000.0
Q_BLOCK = 128
N_EXPERTS = 16
N_GROUPS = 4
EXPERTS_PER_GROUP = N_EXPERTS // N_GROUPS
TOP_K = 2
EXPERT_FF = 1024

kernel_name = 'hybrid_conv_rglru_diffattn_moe_dit'


def _rmsnorm(x, g):
    xf = x.astype(jnp.float32)
    xf = xf * lax.rsqrt(jnp.mean(xf * xf, axis=-1, keepdims=True) + EPS)
    return (xf * g.astype(jnp.float32)).astype(x.dtype)


def _dwconv(u, w, pad):
    return lax.conv_general_dilated(
        u, w[:, None, :].astype(u.dtype), window_strides=(1,), padding=[pad],
        dimension_numbers=('NWC', 'WIO', 'NWC'), feature_group_count=u.shape[-1])


def _linear_scan(a, b, h0=None):
    if h0 is not None:
        b = b.at[:, 0].add(a[:, 0] * h0)

    def combine(left, right):
        a_l, b_l = left
        a_r, b_r = right
        return a_l * a_r, a_r * b_l + b_r

    return lax.associative_scan(combine, (a, b), axis=1)[1]


def _rglru_coeffs(u, wa, ba, wi, bi, lam):
    bsz, n, w = u.shape
    ub = u.reshape(bsz, n, LRU_HEADS, LRU_BLOCK)
    r = jax.nn.sigmoid(jnp.einsum('bshi,hij->bshj', ub, wa).reshape(bsz, n, w) + ba)
    i = jax.nn.sigmoid(jnp.einsum('bshi,hij->bshj', ub, wi).reshape(bsz, n, w) + bi)
    log_a = -LRU_C * r * jax.nn.softplus(-lam)
    a = jnp.exp(log_a)
    b = jnp.sqrt(-jnp.expm1(2.0 * log_a)) * (i * u)
    return a, b


def _bi_rglru(u_ctx, u_lat, wa, ba, wi, bi, lam):
    out_c, out_l = 0.0, 0.0
    for d in range(2):
        uc = u_ctx if d == 0 else u_ctx[:, ::-1]
        ul = u_lat if d == 0 else u_lat[:, ::-1]
        a_c, b_c = _rglru_coeffs(uc, wa[d], ba[d], wi[d], bi[d], lam[d])
        h_c = _linear_scan(a_c, b_c)
        a_l, b_l = _rglru_coeffs(ul, wa[d], ba[d], wi[d], bi[d], lam[d])
        h_l = _linear_scan(a_l, b_l, h_c[:, -1])
        if d == 1:
            h_c, h_l = h_c[:, ::-1], h_l[:, ::-1]
        out_c = out_c + h_c
        out_l = out_l + h_l
    return out_c, out_l


def _conv_lru_mixer(h_ctx, h_lat, w_in, conv_a_w, conv_b_w, conv_b_b,
                    lru_wa, lru_ba, lru_wi, lru_bi, lru_lam, w_out):
    cuts = [CONV_WIDTH, 2 * CONV_WIDTH, 3 * CONV_WIDTH, 3 * CONV_WIDTH + LRU_WIDTH]

    def branches(h):
        gb, gc, xa, yb, xb = jnp.split(h @ w_in, cuts, axis=-1)
        y_a = gb * _dwconv(gc * xa, conv_a_w, (1, 1))
        u = _dwconv(xb, conv_b_w, (2, 1)) + conv_b_b
        return y_a, jax.nn.gelu(yb), u

    ya_c, gate_c, u_c = branches(h_ctx)
    ya_l, gate_l, u_l = branches(h_lat)
    hr_c, hr_l = _bi_rglru(u_c, u_l, lru_wa, lru_ba, lru_wi, lru_bi, lru_lam)
    y_ctx = jnp.concatenate([ya_c, gate_c * hr_c], axis=-1) @ w_out
    y_lat = jnp.concatenate([ya_l, gate_l * hr_l], axis=-1) @ w_out
    return y_ctx, y_lat


def _axial_rope_tables(n_tok, dtype):
    rows = n_tok // GRID_W
    row = jnp.repeat(jnp.arange(rows, dtype=jnp.float32), GRID_W)
    col = jnp.tile(jnp.arange(GRID_W, dtype=jnp.float32), rows)
    inv = ROPE_BASE ** (-jnp.arange(0, ROPE_AXIS_DIM, 2, dtype=jnp.float32) / ROPE_AXIS_DIM)
    ang_r = row[:, None] * inv
    ang_c = col[:, None] * inv
    ang = jnp.concatenate([ang_r, ang_r, ang_c, ang_c], axis=-1)
    return jnp.cos(ang).astype(dtype), jnp.sin(ang).astype(dtype)


def _apply_axial_rope(t, cos, sin):
    half = ROPE_AXIS_DIM // 2
    tr = t.reshape(t.shape[:-1] + (2, 2, half))
    rot = jnp.stack([-tr[..., 1, :], tr[..., 0, :]], axis=-2).reshape(t.shape)
    return t * cos[:, None, None, :] + rot * sin[:, None, None, :]


def _diff_softmax_mix(q, k, v, lam):
    s = jnp.einsum('bqhmd,bkhmd->bhmqk', q, k).astype(jnp.float32) * (DIFF_QK_DIM ** -0.5)
    p = jax.nn.softmax(s, axis=-1)
    w = p[:, :, 0] - lam * p[:, :, 1]
    return jnp.einsum('bhqk,bkhe->bqhe', w.astype(v.dtype), v)


def _diff_head_out(o, g, lam_init, w_out):
    bsz, n = o.shape[0], o.shape[1]
    of = o.astype(jnp.float32)
    of = of * lax.rsqrt(jnp.mean(of * of, axis=-1, keepdims=True) + EPS)
    of = of * g.astype(jnp.float32) * (1.0 - lam_init)
    return of.astype(o.dtype).reshape(bsz, n, DIFF_V_COLS) @ w_out


def _diff_attn_mixer(h_ctx, h_lat, w_in, lq1, lk1, lq2, lk2, subln_g, w_out, lam_init, ctx_out):
    bsz, n, _ = h_lat.shape
    n_ctx = h_ctx.shape[1]
    q, k, v = jnp.split(h_lat @ w_in, [DIFF_QK_COLS, 2 * DIFF_QK_COLS], axis=-1)
    q = q.reshape(bsz, n, DIFF_HEADS, 2, DIFF_QK_DIM)
    k = k.reshape(bsz, n, DIFF_HEADS, 2, DIFF_QK_DIM)
    v = v.reshape(bsz, n, DIFF_HEADS, DIFF_V_DIM)
    cos, sin = _axial_rope_tables(n, h_lat.dtype)
    q = _apply_axial_rope(q, cos, sin)
    k = _apply_axial_rope(k, cos, sin)
    kc, vc = jnp.split(h_ctx @ w_in[:, DIFF_QK_COLS:], [DIFF_QK_COLS], axis=-1)
    kc = kc.reshape(bsz, n_ctx, DIFF_HEADS, 2, DIFF_QK_DIM)
    vc = vc.reshape(bsz, n_ctx, DIFF_HEADS, DIFF_V_DIM)
    k_all = jnp.concatenate([kc, k], axis=1)
    v_all = jnp.concatenate([vc, v], axis=1)
    lam = (jnp.exp(jnp.sum(lq1.astype(jnp.float32) * lk1.astype(jnp.float32)))
           - jnp.exp(jnp.sum(lq2.astype(jnp.float32) * lk2.astype(jnp.float32))) + lam_init)
    n_blk = n // Q_BLOCK
    qb = q.reshape(bsz, n_blk, Q_BLOCK, DIFF_HEADS, 2, DIFF_QK_DIM).swapaxes(0, 1)
    o = lax.map(lambda qi: _diff_softmax_mix(qi, k_all, v_all, lam), qb)
    o = o.swapaxes(0, 1).reshape(bsz, n, DIFF_HEADS, DIFF_V_DIM)
    y_lat = _diff_head_out(o, subln_g, lam_init, w_out)
    y_ctx = None
    if ctx_out:
        qc = (h_ctx @ w_in[:, :DIFF_QK_COLS]).reshape(bsz, n_ctx, DIFF_HEADS, 2, DIFF_QK_DIM)
        y_ctx = _diff_head_out(_diff_softmax_mix(qc, kc, vc, lam), subln_g, lam_init, w_out)
    return y_ctx, y_lat


def _moe(h, router_w, router_b, w_gate, w_up, w_down):
    n_tok = h.shape[0]
    scores = jax.nn.sigmoid((h @ router_w).astype(jnp.float32))
    sel = (scores + router_b.astype(jnp.float32)).reshape(n_tok, N_GROUPS, EXPERTS_PER_GROUP)
    grp_score = jnp.sum(lax.top_k(sel, 2)[0], axis=-1)
    g_idx = jnp.argmax(grp_score, axis=-1)
    in_grp = (jnp.arange(N_GROUPS)[None, :] == g_idx[:, None])[:, :, None]
    masked = jnp.where(in_grp, sel, -jnp.inf).reshape(n_tok, N_EXPERTS)
    _, top_idx = lax.top_k(masked, TOP_K)
    top_s = jnp.take_along_axis(scores, top_idx, axis=-1)
    gates = top_s / jnp.sum(top_s, axis=-1, keepdims=True)
    comb = jnp.sum(jax.nn.one_hot(top_idx, N_EXPERTS, dtype=jnp.float32) * gates[..., None], axis=1)
    comb = comb.astype(h.dtype)
    out = jnp.zeros_like(h)
    for e in range(N_EXPERTS):
        hid = jax.nn.silu(h @ w_gate[e]) * (h @ w_up[e])
        out = out + comb[:, e:e + 1] * (hid @ w_down[e])
    return out


def setup_inputs(seed: int = 0) -> dict:
    key = jax.random.key(seed)
    ks = iter(jax.random.split(key, 40))

    def nrm(shape, scale):
        return jax.random.normal(next(ks), shape, jnp.float32) * scale

    d = D_MODEL
    u = jax.random.uniform(next(ks), (N_EVEN, 2, LRU_WIDTH), jnp.float32, minval=0.9, maxval=0.999)
    a0 = u ** (1.0 / LRU_C)
    return {
        'x': nrm((BATCH, SEQ, d), 1.0),
        'c': nrm((BATCH, d), 1.0),
        'ctx': nrm((BATCH, CTX_LEN, d), 1.0),
        'c_ctx': nrm((d,), 1.0),
        'ada_w': nrm((DEPTH, d, 6 * d), 0.5 * d ** -0.5),
        'ada_b': nrm((DEPTH, 6 * d), 0.01),
        'norm_mix_g': 1.0 + nrm((DEPTH, d), 0.02),
        'norm_ffn_g': 1.0 + nrm((DEPTH, d), 0.02),
        'final_g': 1.0 + nrm((d,), 0.02),
        'w_in_e': nrm((N_EVEN, d, EVEN_IN_COLS), d ** -0.5),
        'conv_a_w': nrm((N_EVEN, CONV_K, CONV_WIDTH), CONV_K ** -0.5),
        'conv_b_w': nrm((N_EVEN, LRU_CONV_K, LRU_WIDTH), LRU_CONV_K ** -0.5),
        'conv_b_b': nrm((N_EVEN, LRU_WIDTH), 0.01),
        'lru_wa': nrm((N_EVEN, 2, LRU_HEADS, LRU_BLOCK, LRU_BLOCK), LRU_BLOCK ** -0.5),
        'lru_ba': nrm((N_EVEN, 2, LRU_WIDTH), 0.01),
        'lru_wi': nrm((N_EVEN, 2, LRU_HEADS, LRU_BLOCK, LRU_BLOCK), LRU_BLOCK ** -0.5),
        'lru_bi': nrm((N_EVEN, 2, LRU_WIDTH), 0.01),
        'lru_lam': jnp.log(a0) - jnp.log1p(-a0),
        'w_out_e': nrm((N_EVEN, CONV_WIDTH + LRU_WIDTH, d), (CONV_WIDTH + LRU_WIDTH) ** -0.5),
        'w_in_o': nrm((N_ODD, d, ODD_IN_COLS), d ** -0.5),
        'lam_q1': nrm((N_ODD, DIFF_QK_DIM), 0.1),
        'lam_k1': nrm((N_ODD, DIFF_QK_DIM), 0.1),
        'lam_q2': nrm((N_ODD, DIFF_QK_DIM), 0.1),
        'lam_k2': nrm((N_ODD, DIFF_QK_DIM), 0.1),
        'subln_g': 1.0 + nrm((N_ODD, DIFF_V_DIM), 0.02),
        'w_out_o': nrm((N_ODD, DIFF_V_COLS, d), DIFF_V_COLS ** -0.5),
        'router_w': nrm((d, N_EXPERTS), d ** -0.5),
        'router_b': nrm((N_EXPERTS,), 0.01),
        'exp_w_gate': nrm((DEPTH, N_EXPERTS, d, EXPERT_FF), d ** -0.5),
        'exp_w_up': nrm((DEPTH, N_EXPERTS, d, EXPERT_FF), d ** -0.5),
        'exp_w_down': nrm((DEPTH, N_EXPERTS, EXPERT_FF, d), EXPERT_FF ** -0.5),
    }


def reference(x, c, ctx, c_ctx, ada_w, ada_b, norm_mix_g, norm_ffn_g, final_g,
              w_in_e, conv_a_w, conv_b_w, conv_b_b, lru_wa, lru_ba, lru_wi, lru_bi, lru_lam, w_out_e,
              w_in_o, lam_q1, lam_k1, lam_q2, lam_k2, subln_g, w_out_o,
              router_w, router_b, exp_w_gate, exp_w_up, exp_w_down):
    d = x.shape[-1]
    silu_c = jax.nn.silu(c)
    silu_cc = jax.nn.silu(c_ctx)
    x_lat, x_ctx = x, ctx
    for l in range(DEPTH):
        last = l == DEPTH - 1
        i = l // 2
        sh1, sc1, g1, sh2, sc2, g2 = jnp.split((silu_c @ ada_w[l] + ada_b[l])[:, None, :], 6, axis=-1)
        sh1c, sc1c, g1c, sh2c, sc2c, g2c = jnp.split(silu_cc @ ada_w[l] + ada_b[l], 6, axis=-1)
        h_lat = _rmsnorm(x_lat, norm_mix_g[l]) * (1.0 + sc1) + sh1
        h_ctx = _rmsnorm(x_ctx, norm_mix_g[l]) * (1.0 + sc1c) + sh1c
        if l % 2 == 0:
            y_ctx, y_lat = _conv_lru_mixer(h_ctx, h_lat, w_in_e[i], conv_a_w[i], conv_b_w[i], conv_b_b[i],
                                           lru_wa[i], lru_ba[i], lru_wi[i], lru_bi[i], lru_lam[i], w_out_e[i])
        else:
            lam_init = 0.8 - 0.6 * math.exp(-0.3 * l)
            y_ctx, y_lat = _diff_attn_mixer(h_ctx, h_lat, w_in_o[i], lam_q1[i], lam_k1[i], lam_q2[i],
                                            lam_k2[i], subln_g[i], w_out_o[i], lam_init, not last)
        x_lat = x_lat + g1 * y_lat
        f_lat = _rmsnorm(x_lat, norm_ffn_g[l]) * (1.0 + sc2) + sh2
        if last:
            y = _moe(f_lat.reshape(-1, d), router_w, router_b, exp_w_gate[l], exp_w_up[l], exp_w_down[l])
            x_lat = x_lat + g2 * y.reshape(x_lat.shape)
        else:
            x_ctx = x_ctx + g1c * y_ctx
            f_ctx = _rmsnorm(x_ctx, norm_ffn_g[l]) * (1.0 + sc2c) + sh2c
            n_ctx_tok = f_ctx.shape[0] * f_ctx.shape[1]
            y = _moe(jnp.concatenate([f_ctx.reshape(-1, d), f_lat.reshape(-1, d)], axis=0),
                     router_w, router_b, exp_w_gate[l], exp_w_up[l], exp_w_down[l])
            x_ctx = x_ctx + g2c * y[:n_ctx_tok].reshape(x_ctx.shape)
            x_lat = x_lat + g2 * y[n_ctx_tok:].reshape(x_lat.shape)
    return _rmsnorm(x_lat, final_g)
```

```python
import functools
import math

import numpy as np
import jax
import jax.numpy as jnp
from jax import lax
from jax.experimental import pallas as pl
from jax.experimental.pallas import tpu as pltpu

F32 = jnp.float32
BF16 = jnp.bfloat16

EPS = 1e-6
GRID_W = 64
LRU_C = 8.0
ROPE_BASE = 10000.0
N_GROUPS = 4
EXPERTS_PER_GROUP = 4
PAIRS_PER_GROUP = 6
N_CLASSES = N_GROUPS * PAIRS_PER_GROUP
LANES = 128
SUBLANES = 8
TOKEN_TILE = 256
EXPERT_TILE = 256
SCAN_CHUNK = 128
VMEM_LIMIT_BYTES = 56 * 1024 * 1024

_PAIRS = [(0, 1), (0, 2), (0, 3), (1, 2), (1, 3), (2, 3)]
_CLASS_LO = np.array([4 * g + i for g in range(N_GROUPS) for (i, j) in _PAIRS], np.int32)
_CLASS_HI = np.array([4 * g + j for g in range(N_GROUPS) for (i, j) in _PAIRS], np.int32)


def _cparams(*sem):
    return pltpu.CompilerParams(dimension_semantics=sem, vmem_limit_bytes=VMEM_LIMIT_BYTES)


def _sigmoid(v):
    return 1.0 / (1.0 + jnp.exp(-v))


def _dot(a, b):
    return jnp.dot(a, b, preferred_element_type=F32)


def _mod_norm(xv, g, sc, sh):
    ms = jnp.mean(xv * xv, axis=-1, keepdims=True)
    return xv * lax.rsqrt(ms + EPS) * g * (1.0 + sc) + sh


def _ada_kernel(c_ref, w_ref, b_ref, o_ref):
    cv = c_ref[...]
    s = (cv * _sigmoid(cv)).astype(BF16)
    o_ref[0] = _dot(s, w_ref[0].astype(BF16)) + b_ref[0]


def _ada(cond, ada_w, ada_b):
    depth, d, n6 = ada_w.shape
    rows = cond.shape[0]
    tn = min(1024, n6)
    return pl.pallas_call(
        _ada_kernel,
        grid=(depth, n6 // tn),
        in_specs=[pl.BlockSpec((rows, d), lambda l, j: (0, 0)),
                  pl.BlockSpec((1, d, tn), lambda l, j: (l, 0, j)),
                  pl.BlockSpec((1, 1, tn), lambda l, j: (l, 0, j))],
        out_specs=pl.BlockSpec((1, rows, tn), lambda l, j: (l, 0, j)),
        out_shape=jax.ShapeDtypeStruct((depth, rows, n6), F32),
        compiler_params=_cparams("arbitrary", "arbitrary"),
        name="ada_modulation",
    )(cond, ada_w, ada_b.reshape(depth, 1, n6))


def _prenorm_kernel(x_ref, ctx_ref, mod_ref, g_ref, o_ref, *, n_ctx_tiles):
    i = pl.program_id(1)
    mod = mod_ref[0, 0]

    def run(src_ref):
        o_ref[0] = _mod_norm(src_ref[0], g_ref[...], mod[1:2], mod[0:1]).astype(o_ref.dtype)

    @pl.when(i < n_ctx_tiles)
    def _():
        run(ctx_ref)

    @pl.when(i >= n_ctx_tiles)
    def _():
        run(x_ref)


def _dual_specs(bsz, tm, d, nct):
    x_spec = pl.BlockSpec((1, tm, d), lambda b, i: (b, jnp.maximum(i - nct, 0), 0))
    c_spec = pl.BlockSpec((1, tm, d), lambda b, i: (b, jnp.minimum(i, nct - 1), 0))
    return x_spec, c_spec


def _mod_spec(layer, bsz, nct, d):
    return pl.BlockSpec((1, 1, 6, d), lambda b, i: (layer, jnp.where(i < nct, bsz, b), 0, 0))


def _prenorm(x, ctx, mod, g, layer):
    bsz, n_lat, d = x.shape
    n_ctx = ctx.shape[1]
    tm = TOKEN_TILE
    nct = n_ctx // tm
    t_all = n_ctx + n_lat
    x_spec, c_spec = _dual_specs(bsz, tm, d, nct)
    return pl.pallas_call(
        functools.partial(_prenorm_kernel, n_ctx_tiles=nct),
        grid=(bsz, t_all // tm),
        in_specs=[x_spec, c_spec, _mod_spec(layer, bsz, nct, d),
                  pl.BlockSpec((1, d), lambda b, i: (0, 0))],
        out_specs=pl.BlockSpec((1, tm, d), lambda b, i: (b, i, 0)),
        out_shape=jax.ShapeDtypeStruct((bsz, t_all, d), BF16),
        compiler_params=_cparams("arbitrary", "arbitrary"),
        name="prenorm0",
    )(x, ctx, mod, g[layer:layer + 1])


def _tile_scan(a, b, row, reverse):
    for s in (1, 2, 4):
        if reverse:
            a_s = pltpu.roll(a, SUBLANES - s, 0)
            b_s = pltpu.roll(b, SUBLANES - s, 0)
            ok = row < SUBLANES - s
        else:
            a_s = pltpu.roll(a, s, 0)
            b_s = pltpu.roll(b, s, 0)
            ok = row >= s
        a_s = jnp.where(ok, a_s, 1.0)
        b_s = jnp.where(ok, b_s, 0.0)
        b = a * b_s + b
        a = a * a_s
    return a, b


def _gelu_tanh(v):
    return 0.5 * v * (1.0 + jnp.tanh(math.sqrt(2.0 / math.pi) * (v + 0.044715 * (v * v * v))))


def _lru_coeffs(u, gw, ba, bi, lam):
    gates = _dot(u.astype(BF16), gw)
    r = _sigmoid(gates[:, :LANES] + ba)
    ig = _sigmoid(gates[:, LANES:] + bi)
    log_a = (-LRU_C * jnp.log1p(jnp.exp(-lam))) * r
    a = jnp.exp(log_a)
    b = jnp.sqrt(1.0 - a * a) * (ig * u)
    return a, b


def _mixer_kernel(h_ref, w_ref, gw_ref, p_ref, za_ref, zb_ref, y_s, u_s, g_s, hf_s,
                  *, t_all, n_ctx, mm_rows):
    ch = SCAN_CHUNK
    n_chunk = t_all // ch
    halo = SUBLANES
    zero_halo = jnp.zeros((halo, y_s.shape[1]), F32)
    y_s[pl.ds(0, halo), :] = zero_halo
    y_s[pl.ds(halo + t_all, halo), :] = zero_halo

    def mm(c, carry):
        r0 = pl.multiple_of(c * mm_rows, mm_rows)
        y_s[pl.ds(halo + r0, mm_rows), :] = _dot(h_ref[0, pl.ds(r0, mm_rows), :], w_ref[0])
        return carry

    lax.fori_loop(0, t_all // mm_rows, mm, 0)

    prm = p_ref[0]
    ca = [prm[k:k + 1] for k in range(3)]
    cb = [prm[3 + k:4 + k] for k in range(4)]
    cb_bias = prm[7:8]
    row8 = lax.broadcasted_iota(jnp.int32, (SUBLANES, LANES), 0)
    ext = ch + 2 * halo

    def shifted(v_ext, back):
        return pltpu.roll(v_ext, back % ext, 0)[halo:halo + ch]

    def fwd_chunk(c, hcar):
        r0 = pl.multiple_of(c * ch, ch)
        rows = r0 + lax.broadcasted_iota(jnp.int32, (ch, LANES), 0)
        first = (rows == 0) | (rows == n_ctx)
        second = (rows == 1) | (rows == n_ctx + 1)
        last = (rows == n_ctx - 1) | (rows == t_all - 1)
        prod = y_s[pl.ds(r0, ext), LANES:2 * LANES] * y_s[pl.ds(r0, ext), 2 * LANES:3 * LANES]
        conv = (ca[0] * jnp.where(first, 0.0, shifted(prod, 1))
                + ca[1] * prod[halo:halo + ch]
                + ca[2] * jnp.where(last, 0.0, shifted(prod, -1)))
        za_ref[0, pl.ds(r0, ch), :] = (y_s[pl.ds(halo + r0, ch), 0:LANES] * conv).astype(za_ref.dtype)
        g_s[pl.ds(r0, ch), :] = _gelu_tanh(y_s[pl.ds(halo + r0, ch), 3 * LANES:4 * LANES])
        xb = y_s[pl.ds(r0, ext), 4 * LANES:5 * LANES]
        u = (cb[0] * jnp.where(first | second, 0.0, shifted(xb, 2))
             + cb[1] * jnp.where(first, 0.0, shifted(xb, 1))
             + cb[2] * xb[halo:halo + ch]
             + cb[3] * jnp.where(last, 0.0, shifted(xb, -1))
             + cb_bias)
        u_s[pl.ds(r0, ch), :] = u
        a, b = _lru_coeffs(u, gw_ref[0, 0], prm[8:9], prm[9:10], prm[10:11])
        for v in range(ch // SUBLANES):
            at, bt = _tile_scan(a[v * SUBLANES:(v + 1) * SUBLANES], b[v * SUBLANES:(v + 1) * SUBLANES],
                                row8, False)
            hv = at * hcar + bt
            hf_s[pl.ds(r0 + v * SUBLANES, SUBLANES), :] = hv
            hcar = jnp.broadcast_to(hv[SUBLANES - 1:SUBLANES], (SUBLANES, LANES))
        return hcar

    lax.fori_loop(0, n_chunk, fwd_chunk, jnp.zeros((SUBLANES, LANES), F32))

    n_ctx_chunk = n_ctx // ch

    def bwd_chunk(k, hcar):
        c = jnp.where(k < n_ctx_chunk, n_ctx_chunk - 1 - k, n_chunk - 1 - (k - n_ctx_chunk))
        r0 = pl.multiple_of(c * ch, ch)
        u = u_s[pl.ds(r0, ch), :]
        a, b = _lru_coeffs(u, gw_ref[0, 1], prm[11:12], prm[12:13], prm[13:14])
        tiles = [None] * (ch // SUBLANES)
        for v in reversed(range(ch // SUBLANES)):
            at, bt = _tile_scan(a[v * SUBLANES:(v + 1) * SUBLANES], b[v * SUBLANES:(v + 1) * SUBLANES],
                                row8, True)
            hv = at * hcar + bt
            tiles[v] = hv
            hcar = jnp.broadcast_to(hv[0:1], (SUBLANES, LANES))
        hb = jnp.concatenate(tiles, axis=0)
        zb_ref[0, pl.ds(r0, ch), :] = (g_s[pl.ds(r0, ch), :] * (hf_s[pl.ds(r0, ch), :] + hb)).astype(zb_ref.dtype)
        return hcar

    lax.fori_loop(0, n_chunk, bwd_chunk, jnp.zeros((SUBLANES, LANES), F32))


def _mixer0(h, w_in, conv_a_w, conv_b_w, conv_b_b, lru_wa, lru_ba, lru_wi, lru_bi, lru_lam, n_ctx):
    bsz, t_all, d = h.shape
    width = conv_a_w.shape[-1]
    heads = width // LANES
    assert lru_wa.shape[1] == heads and lru_wa.shape[2] == LANES
    w_r = w_in.reshape(d, 5, heads, LANES).transpose(2, 0, 1, 3).reshape(heads, d, 5 * LANES).astype(BF16)
    gw = jnp.concatenate([lru_wa, lru_wi], axis=-1).transpose(1, 0, 2, 3).astype(BF16)

    def per_head(v):
        return v.reshape(-1, heads, LANES).transpose(1, 0, 2)

    prm = jnp.concatenate([
        per_head(conv_a_w), per_head(conv_b_w), per_head(conv_b_b[None]),
        per_head(lru_ba[0:1]), per_head(lru_bi[0:1]), per_head(lru_lam[0:1]),
        per_head(lru_ba[1:2]), per_head(lru_bi[1:2]), per_head(lru_lam[1:2]),
        jnp.zeros((heads, 2, LANES), F32)], axis=1)
    mm_rows = TOKEN_TILE
    out_sds = jax.ShapeDtypeStruct((bsz, t_all, width), BF16)
    out_spec = pl.BlockSpec((1, t_all, LANES), lambda b, j: (b, 0, j))
    return pl.pallas_call(
        functools.partial(_mixer_kernel, t_all=t_all, n_ctx=n_ctx, mm_rows=mm_rows),
        grid=(bsz, heads),
        in_specs=[pl.BlockSpec((1, t_all, d), lambda b, j: (b, 0, 0)),
                  pl.BlockSpec((1, d, 5 * LANES), lambda b, j: (j, 0, 0)),
                  pl.BlockSpec((1, 2, LANES, 2 * LANES), lambda b, j: (j, 0, 0, 0)),
                  pl.BlockSpec((1, 16, LANES), lambda b, j: (j, 0, 0))],
        out_specs=[out_spec, out_spec],
        out_shape=[out_sds, out_sds],
        scratch_shapes=[pltpu.VMEM((t_all + 2 * SUBLANES, 5 * LANES), F32),
                        pltpu.VMEM((t_all, LANES), F32),
                        pltpu.VMEM((t_all, LANES), F32),
                        pltpu.VMEM((t_all, LANES), F32)],
        compiler_params=_cparams("arbitrary", "arbitrary"),
        name="mixer0_conv_lru",
    )(h, w_r, gw, prm)


def _route(f, rw_ref, rb_ref, carry_ref, info_ref, cnt_ref, tm):
    f_hi = f.astype(BF16)
    f_lo = (f - f_hi.astype(F32)).astype(BF16)
    rw = rw_ref[...]
    rw_hi = rw.astype(BF16)
    rw_lo = (rw - rw_hi.astype(F32)).astype(BF16)
    logits = _dot(f_hi, rw_hi) + _dot(f_lo, rw_hi) + _dot(f_hi, rw_lo)
    lt = logits.T
    n_exp = N_GROUPS * EXPERTS_PER_GROUP
    scores = _sigmoid(lt[0:n_exp])
    sel = scores + rb_ref[...]
    s_rows = [scores[e:e + 1] for e in range(n_exp)]
    v_rows = [sel[e:e + 1] for e in range(n_exp)]
    best = None
    gidx = None
    for g in range(N_GROUPS):
        a, b, c, d = v_rows[4 * g:4 * g + 4]
        hi1, lo1 = jnp.maximum(a, b), jnp.minimum(a, b)
        hi2, lo2 = jnp.maximum(c, d), jnp.minimum(c, d)
        gs = jnp.maximum(hi1, hi2) + jnp.maximum(jnp.minimum(hi1, hi2), jnp.maximum(lo1, lo2))
        if g == 0:
            best, gidx = gs, jnp.zeros(gs.shape, jnp.int32)
        else:
            better = gs > best
            gidx = jnp.where(better, g, gidx)
            best = jnp.where(better, gs, best)
    found = None
    for e in range(n_exp):
        g = e // EXPERTS_PER_GROUP
        rank = jnp.zeros(best.shape, jnp.int32)
        for e2 in range(4 * g, 4 * g + 4):
            if e2 == e:
                continue
            ahead = v_rows[e2] > v_rows[e]
            if e2 < e:
                ahead = ahead | (v_rows[e2] == v_rows[e])
            rank = rank + ahead.astype(jnp.int32)
        chosen = (gidx == g) & (rank < 2)
        if e == 0:
            found = chosen
            lo_idx = jnp.zeros(best.shape, jnp.int32)
            hi_idx = jnp.zeros(best.shape, jnp.int32)
            lo_s = s_rows[0]
            hi_s = s_rows[0]
        else:
            take_lo = chosen & jnp.logical_not(found)
            lo_idx = jnp.where(take_lo, e, lo_idx)
            lo_s = jnp.where(take_lo, s_rows[e], lo_s)
            hi_idx = jnp.where(chosen, e, hi_idx)
            hi_s = jnp.where(chosen, s_rows[e], hi_s)
            found = found | chosen
    denom = lo_s + hi_s
    gate_lo = lo_s / denom
    gate_hi = hi_s / denom
    li = lo_idx - 4 * gidx
    hj = hi_idx - 4 * gidx
    pair = jnp.where(li == 0, 0, jnp.where(li == 1, 3, 5)) + hj - li - 1
    cls = PAIRS_PER_GROUP * gidx + pair
    n_cls_pad = 32
    onehot = (lax.broadcasted_iota(jnp.int32, (n_cls_pad, tm), 0) == cls).astype(F32)
    tri = (lax.broadcasted_iota(jnp.int32, (tm, tm), 0)
           < lax.broadcasted_iota(jnp.int32, (tm, tm), 1)).astype(BF16)
    excl = _dot(onehot.astype(BF16), tri)
    rank_in_cls = jnp.sum(onehot * (excl + carry_ref[:, 0:1]), axis=0, keepdims=True)
    carry_ref[...] = carry_ref[...] + jnp.sum(onehot, axis=1, keepdims=True)
    cnt_ref[...] = carry_ref[...]
    rid = lax.broadcasted_iota(jnp.int32, (LANES, tm), 0)
    packed = jnp.where(rid == 0, cls.astype(F32),
                       jnp.where(rid == 1, gate_lo,
                                 jnp.where(rid == 2, gate_hi,
                                           jnp.where(rid == 3, rank_in_cls, 0.0))))
    info_ref[...] = packed.T


def _post_mixer(xin, y, mod_ref, g_ref, rw_ref, rb_ref, x1_ref, f_ref, info_ref, cnt_ref, carry_ref, tm):
    first = (pl.program_id(0) == 0) & (pl.program_id(1) == 0)

    @pl.when(first)
    def _():
        carry_ref[...] = jnp.zeros(carry_ref.shape, F32)

    mod = mod_ref[0, 0]
    x1 = xin + mod[2:3] * y
    x1_ref[0] = x1
    f = _mod_norm(x1, g_ref[...], mod[4:5], mod[3:4])
    f_ref[0] = f
    _route(f, rw_ref, rb_ref, carry_ref, info_ref, cnt_ref, tm)


def _outproj0_kernel(za_ref, zb_ref, wa_ref, wb_ref, x_ref, ctx_ref, mod_ref, g_ref, rw_ref, rb_ref,
                     x1_ref, f_ref, info_ref, cnt_ref, carry_ref, *, n_ctx_tiles, tm):
    y = _dot(za_ref[0], wa_ref[...]) + _dot(zb_ref[0], wb_ref[...])
    i = pl.program_id(1)
    xin = jnp.where(i < n_ctx_tiles, ctx_ref[0], x_ref[0])
    _post_mixer(xin, y, mod_ref, g_ref, rw_ref, rb_ref, x1_ref, f_ref, info_ref, cnt_ref, carry_ref, tm)


def _outproj1_kernel(z_ref, w_ref, x_ref, mod_ref, g_ref, rw_ref, rb_ref,
                     x1_ref, f_ref, info_ref, cnt_ref, carry_ref, *, tm):
    y = _dot(z_ref[0], w_ref[...])
    _post_mixer(x_ref[0], y, mod_ref, g_ref, rw_ref, rb_ref, x1_ref, f_ref, info_ref, cnt_ref, carry_ref, tm)


def _router_operands(router_w, router_b):
    d, n_exp = router_w.shape
    rw = jnp.zeros((d, LANES), F32).at[:, :n_exp].set(router_w)
    return rw, router_b.reshape(n_exp, 1)


def _post_outs(bsz, tiles, tm, d):
    n_tok = bsz * tiles * tm
    out_specs = [pl.BlockSpec((1, tm, d), lambda b, i: (b, i, 0)),
                 pl.BlockSpec((1, tm, d), lambda b, i: (b, i, 0)),
                 pl.BlockSpec((tm, LANES), lambda b, i: (b * tiles + i, 0)),
                 pl.BlockSpec((32, LANES), lambda b, i: (0, 0))]
    out_shape = [jax.ShapeDtypeStruct((bsz, tiles * tm, d), F32),
                 jax.ShapeDtypeStruct((bsz, tiles * tm, d), F32),
                 jax.ShapeDtypeStruct((n_tok, LANES), F32),
                 jax.ShapeDtypeStruct((32, LANES), F32)]
    return out_specs, out_shape


def _outproj0(za, zb, w_out, x, ctx, mod, g_ffn, router_w, router_b, layer):
    bsz, t_all, width = za.shape
    d = x.shape[-1]
    tm = TOKEN_TILE
    nct = ctx.shape[1] // tm
    tiles = t_all // tm
    w_bf = w_out.astype(BF16)
    rw, rb = _router_operands(router_w, router_b)
    x_spec, c_spec = _dual_specs(bsz, tm, d, nct)
    z_spec = pl.BlockSpec((1, tm, width), lambda b, i: (b, i, 0))
    out_specs, out_shape = _post_outs(bsz, tiles, tm, d)
    return pl.pallas_call(
        functools.partial(_outproj0_kernel, n_ctx_tiles=nct, tm=tm),
        grid=(bsz, tiles),
        in_specs=[z_spec, z_spec,
                  pl.BlockSpec((width, d), lambda b, i: (0, 0)),
                  pl.BlockSpec((width, d), lambda b, i: (1, 0)),
                  x_spec, c_spec, _mod_spec(layer, bsz, nct, d),
                  pl.BlockSpec((1, d), lambda b, i: (0, 0)),
                  pl.BlockSpec((d, LANES), lambda b, i: (0, 0)),
                  pl.BlockSpec(rb.shape, lambda b, i: (0, 0))],
        out_specs=out_specs,
        out_shape=out_shape,
        scratch_shapes=[pltpu.VMEM((32, LANES), F32)],
        compiler_params=_cparams("arbitrary", "arbitrary"),
        name="outproj0_router",
    )(za, zb, w_bf, w_bf, x, ctx, mod, g_ffn[layer:layer + 1], rw, rb)


def _outproj1(z, w_out, x_all, mod, g_ffn, router_w, router_b, layer, n_ctx):
    bsz, n_lat, width = z.shape
    d = x_all.shape[-1]
    tm = TOKEN_TILE
    nct = n_ctx // tm
    tiles = n_lat // tm
    rw, rb = _router_operands(router_w, router_b)
    out_specs, out_shape = _post_outs(bsz, tiles, tm, d)
    return pl.pallas_call(
        functools.partial(_outproj1_kernel, tm=tm),
        grid=(bsz, tiles),
        in_specs=[pl.BlockSpec((1, tm, width), lambda b, i: (b, i, 0)),
                  pl.BlockSpec((width, d), lambda b, i: (0, 0)),
                  pl.BlockSpec((1, tm, d), lambda b, i: (b, i + nct, 0)),
                  pl.BlockSpec((1, 1, 6, d), lambda b, i: (layer, b, 0, 0)),
                  pl.BlockSpec((1, d), lambda b, i: (0, 0)),
                  pl.BlockSpec((d, LANES), lambda b, i: (0, 0)),
                  pl.BlockSpec(rb.shape, lambda b, i: (0, 0))],
        out_specs=out_specs,
        out_shape=out_shape,
        scratch_shapes=[pltpu.VMEM((32, LANES), F32)],
        compiler_params=_cparams("arbitrary", "arbitrary"),
        name="outproj1_router",
    )(z, w_out.astype(BF16), x_all, mod, g_ffn[layer:layer + 1], rw, rb)


def _dispatch_plan(info, counts, n_tok, tme):
    cls = info[:, 0].astype(jnp.int32)
    rank = info[:, 3].astype(jnp.int32)
    cnt = counts[:N_CLASSES, 0].astype(jnp.int32)
    padded = ((cnt + tme - 1) // tme) * tme
    ends = jnp.cumsum(padded)
    starts = ends - padded
    pos = starts[cls] + rank
    r_pad = n_tok + N_CLASSES * tme
    src = jnp.zeros((r_pad,), jnp.int32).at[pos].set(jnp.arange(n_tok, dtype=jnp.int32))
    n_tiles = r_pad // tme
    cls_tiles = padded // tme
    cls_tile_start = starts // tme
    seg_cls = np.array([[c for c in range(N_CLASSES) if _CLASS_LO[c] == e or _CLASS_HI[c] == e]
                        for e in range(N_GROUPS * EXPERTS_PER_GROUP)], np.int32).reshape(-1)
    seg_exp = np.repeat(np.arange(N_GROUPS * EXPERTS_PER_GROUP, dtype=np.int32), 3)
    seg_slot = (_CLASS_HI[seg_cls] == seg_exp).astype(np.int32)
    seg_len = cls_tiles[seg_cls]
    seg_end = jnp.cumsum(seg_len)
    seg_start = seg_end - seg_len
    total = seg_end[-1]
    n_steps = 2 * n_tiles
    q = jnp.arange(n_steps, dtype=jnp.int32)
    qc = jnp.minimum(q, total - 1)
    seg = jnp.searchsorted(seg_end, qc, side="right").astype(jnp.int32)
    real = q < total
    spare = q - total
    step_tile = jnp.where(real, cls_tile_start[seg_cls][seg] + (qc - seg_start[seg]), total // 2 + spare // 2)
    step_exp = jnp.asarray(seg_exp)[seg]
    step_slot = jnp.where(real, jnp.asarray(seg_slot)[seg], spare % 2)
    step_valid = real.astype(jnp.int32)
    return pos, src, r_pad, (step_exp, step_tile, step_slot, step_valid)


def _gather_rows_kernel(idx_ref, src_hbm, o_ref, buf, sem, *, rows):
    def copy(r):
        return pltpu.make_async_copy(src_hbm.at[pl.ds(idx_ref[0, 0, r], 1), :], buf.at[pl.ds(r, 1), :], sem)

    def issue(r, carry):
        copy(r).start()
        return carry

    def drain(r, carry):
        copy(r).wait()
        return carry

    lax.fori_loop(0, rows, issue, 0)
    lax.fori_loop(0, rows, drain, 0)
    o_ref[...] = buf[...].astype(o_ref.dtype)


def _gather_sorted(f_flat, src, tme):
    n_tok, d = f_flat.shape
    r_pad = src.shape[0]
    n_tiles = r_pad // tme
    return pl.pallas_call(
        functools.partial(_gather_rows_kernel, rows=tme),
        grid=(n_tiles,),
        in_specs=[pl.BlockSpec((1, 1, tme), lambda i: (i, 0, 0), memory_space=pltpu.SMEM),
                  pl.BlockSpec(memory_space=pl.ANY)],
        out_specs=pl.BlockSpec((tme, d), lambda i: (i, 0)),
        out_shape=jax.ShapeDtypeStruct((r_pad, d), BF16),
        scratch_shapes=[pltpu.VMEM((tme, d), F32), pltpu.SemaphoreType.DMA],
        compiler_params=_cparams("arbitrary"),
        name="moe_gather_sorted",
    )(src.reshape(n_tiles, 1, tme), f_flat)


def _expert_kernel(exp_ref, tile_ref, slot_ref, valid_ref, x_ref, wg_ref, wu_ref, wd_ref, o_ref):
    s = pl.program_id(0)

    @pl.when(valid_ref[s] == 1)
    def _():
        xv = x_ref[...]
        hg = _dot(xv, wg_ref[0])
        hu = _dot(xv, wu_ref[0])
        hid = (hg * _sigmoid(hg) * hu).astype(BF16)
        o_ref[...] = _dot(hid, wd_ref[0])

    @pl.when(valid_ref[s] == 0)
    def _():
        o_ref[...] = jnp.zeros(o_ref.shape, o_ref.dtype)


def _experts(xs, plan, w_gate, w_up, w_down, tme):
    r_pad, d = xs.shape
    n_exp, _, ff = w_gate.shape
    step_exp, step_tile, step_slot, step_valid = plan
    n_steps = step_exp.shape[0]
    grid_spec = pltpu.PrefetchScalarGridSpec(
        num_scalar_prefetch=4,
        grid=(n_steps,),
        in_specs=[pl.BlockSpec((tme, d), lambda s, e, t, w, v: (t[s], 0)),
                  pl.BlockSpec((1, d, ff), lambda s, e, t, w, v: (e[s], 0, 0)),
                  pl.BlockSpec((1, d, ff), lambda s, e, t, w, v: (e[s], 0, 0)),
                  pl.BlockSpec((1, ff, d), lambda s, e, t, w, v: (e[s], 0, 0))],
        out_specs=pl.BlockSpec((tme, d), lambda s, e, t, w, v: (t[s], w[s])),
    )
    return pl.pallas_call(
        _expert_kernel,
        grid_spec=grid_spec,
        out_shape=jax.ShapeDtypeStruct((r_pad, 2 * d), F32),
        compiler_params=_cparams("arbitrary"),
        name="moe_experts",
    )(step_exp, step_tile, step_slot, step_valid, xs,
      w_gate.astype(BF16), w_up.astype(BF16), w_down.astype(BF16))


def _combine_rows(idx_ref, o_hbm, buf, sem, rows):
    def copy(r):
        return pltpu.make_async_copy(o_hbm.at[pl.ds(idx_ref[0, 0, r], 1), :], buf.at[pl.ds(r, 1), :], sem)

    def issue(r, carry):
        copy(r).start()
        return carry

    def drain(r, carry):
        copy(r).wait()
        return carry

    lax.fori_loop(0, rows, issue, 0)
    lax.fori_loop(0, rows, drain, 0)


def _moe_residual(buf, info_ref, x1_ref, mod_ref, d):
    info = info_ref[...]
    y = info[:, 1:2] * buf[:, 0:d] + info[:, 2:3] * buf[:, d:2 * d]
    return x1_ref[0] + mod_ref[0, 0][5:6] * y


def _combine0_kernel(idx_ref, o_hbm, info_ref, x1_ref, mod_ref, modn_ref, g_ref, x2_ref, h_ref, buf, sem,
                     *, rows, d):
    _combine_rows(idx_ref, o_hbm, buf, sem, rows)
    x2 = _moe_residual(buf, info_ref, x1_ref, mod_ref, d)
    x2_ref[0] = x2
    modn = modn_ref[0, 0]
    h_ref[0] = _mod_norm(x2, g_ref[...], modn[1:2], modn[0:1]).astype(h_ref.dtype)


def _combine1_kernel(idx_ref, o_hbm, info_ref, x1_ref, mod_ref, g_ref, out_ref, buf, sem, *, rows, d):
    _combine_rows(idx_ref, o_hbm, buf, sem, rows)
    x2 = _moe_residual(buf, info_ref, x1_ref, mod_ref, d)
    ms = jnp.mean(x2 * x2, axis=-1, keepdims=True)
    out_ref[0] = x2 * lax.rsqrt(ms + EPS) * g_ref[...]


def _combine0(o_sorted, pos, info, x1, mod, norm_mix_g, layer, n_ctx):
    bsz, t_all, d = x1.shape
    tm = TOKEN_TILE
    tiles = t_all // tm
    nct = n_ctx // tm
    tok_spec = pl.BlockSpec((1, tm, d), lambda b, i: (b, i, 0))
    return pl.pallas_call(
        functools.partial(_combine0_kernel, rows=tm, d=d),
        grid=(bsz, tiles),
        in_specs=[pl.BlockSpec((1, 1, tm), lambda b, i: (b * tiles + i, 0, 0), memory_space=pltpu.SMEM),
                  pl.BlockSpec(memory_space=pl.ANY),
                  pl.BlockSpec((tm, LANES), lambda b, i: (b * tiles + i, 0)),
                  tok_spec,
                  _mod_spec(layer, bsz, nct, d),
                  _mod_spec(layer + 1, bsz, nct, d),
                  pl.BlockSpec((1, d), lambda b, i: (0, 0))],
        out_specs=[tok_spec, tok_spec],
        out_shape=[jax.ShapeDtypeStruct((bsz, t_all, d), F32), jax.ShapeDtypeStruct((bsz, t_all, d), BF16)],
        scratch_shapes=[pltpu.VMEM((tm, 2 * d), F32), pltpu.SemaphoreType.DMA],
        compiler_params=_cparams("arbitrary", "arbitrary"),
        name="moe_combine0",
    )(pos.reshape(bsz * tiles, 1, tm), o_sorted, info, x1, mod, mod, norm_mix_g[layer + 1:layer + 2])


def _combine1(o_sorted, pos, info, x1, mod, final_g, layer):
    bsz, n_lat, d = x1.shape
    tm = TOKEN_TILE
    tiles = n_lat // tm
    tok_spec = pl.BlockSpec((1, tm, d), lambda b, i: (b, i, 0))
    return pl.pallas_call(
        functools.partial(_combine1_kernel, rows=tm, d=d),
        grid=(bsz, tiles),
        in_specs=[pl.BlockSpec((1, 1, tm), lambda b, i: (b * tiles + i, 0, 0), memory_space=pltpu.SMEM),
                  pl.BlockSpec(memory_space=pl.ANY),
                  pl.BlockSpec((tm, LANES), lambda b, i: (b * tiles + i, 0)),
                  tok_spec,
                  pl.BlockSpec((1, 1, 6, d), lambda b, i: (layer, b, 0, 0)),
                  pl.BlockSpec((1, d), lambda b, i: (0, 0))],
        out_specs=tok_spec,
        out_shape=jax.ShapeDtypeStruct((bsz, n_lat, d), F32),
        scratch_shapes=[pltpu.VMEM((tm, 2 * d), F32), pltpu.SemaphoreType.DMA],
        compiler_params=_cparams("arbitrary", "arbitrary"),
        name="moe_combine1",
    )(pos.reshape(bsz * tiles, 1, tm), o_sorted, info, x1, mod, final_g.reshape(1, d))


def _moe_sorted_outputs(f, info, counts, w_gate, w_up, w_down):
    d = f.shape[-1]
    f_flat = f.reshape(-1, d)
    n_tok = f_flat.shape[0]
    tme = EXPERT_TILE
    pos, src, _, plan = _dispatch_plan(info, counts, n_tok, tme)
    xs = _gather_sorted(f_flat, src, tme)
    return _experts(xs, plan, w_gate, w_up, w_down, tme), pos


def _rope_tables(n_lat, qk_dim):
    axis_dim = qk_dim // 2
    half = axis_dim // 2
    rows = n_lat // GRID_W
    row = jnp.repeat(jnp.arange(rows, dtype=F32), GRID_W)
    col = jnp.tile(jnp.arange(GRID_W, dtype=F32), rows)
    inv = ROPE_BASE ** (-jnp.arange(0, axis_dim, 2, dtype=F32) / axis_dim)
    ang_r = row[:, None] * inv
    ang_c = col[:, None] * inv
    ang = jnp.concatenate([ang_r, ang_r, ang_c, ang_c], axis=-1)
    cos, sin = jnp.cos(ang), jnp.sin(ang)
    lower = (jnp.arange(qk_dim) % axis_dim) < half
    sin_up = jnp.where(lower, -sin, 0.0)
    sin_dn = jnp.where(lower, 0.0, sin)
    return cos, sin_up, sin_dn


def _qkv_kernel(h_ref, w_ref, cos_ref, su_ref, sd_ref, o_ref, *, n_ctx_tiles, qk_dim, scale):
    j = pl.program_id(0)
    i = pl.program_id(2)
    y = _dot(h_ref[0], w_ref[0])
    rope = (j < 2) & (i >= n_ctx_tiles)

    @pl.when(jnp.logical_not(rope))
    def _():
        o_ref[0, 0] = y.astype(o_ref.dtype)

    @pl.when(rope)
    def _():
        half = qk_dim // 4
        cos, su, sd = cos_ref[...], su_ref[...], sd_ref[...]
        mult = jnp.where(j == 0, scale, 1.0)
        for k in range(y.shape[1] // qk_dim):
            t = y[:, k * qk_dim:(k + 1) * qk_dim]
            up = pltpu.roll(t, qk_dim - half, 1)
            dn = pltpu.roll(t, half, 1)
            o_ref[0, 0, :, k * qk_dim:(k + 1) * qk_dim] = ((t * cos + up * su + dn * sd) * mult).astype(o_ref.dtype)


def _qkv(h, w_in_o, n_ctx, qk_dim):
    bsz, t_all, d = h.shape
    cols = w_in_o.shape[-1] // 3
    n_lat = t_all - n_ctx
    tm = TOKEN_TILE
    nct = n_ctx // tm
    w3 = w_in_o.reshape(d, 3, cols).transpose(1, 0, 2).astype(BF16)
    cos, su, sd = _rope_tables(n_lat, qk_dim)
    tab_spec = pl.BlockSpec((tm, qk_dim), lambda j, b, i: (jnp.maximum(i - nct, 0), 0))
    return pl.pallas_call(
        functools.partial(_qkv_kernel, n_ctx_tiles=nct, qk_dim=qk_dim, scale=qk_dim ** -0.5),
        grid=(3, bsz, t_all // tm),
        in_specs=[pl.BlockSpec((1, tm, d), lambda j, b, i: (b, i, 0)),
                  pl.BlockSpec((1, d, cols), lambda j, b, i: (j, 0, 0)),
                  tab_spec, tab_spec, tab_spec],
        out_specs=pl.BlockSpec((1, 1, tm, cols), lambda j, b, i: (j, b, i, 0)),
        out_shape=jax.ShapeDtypeStruct((3, bsz, t_all, cols), BF16),
        compiler_params=_cparams("arbitrary", "arbitrary", "arbitrary"),
        name="qkv_rope",
    )(h, w3, cos, su, sd)


def _attn_kernel(q_ref, k_ref, v_ref, lp_ref, g_ref, o_ref, *, qk_dim, lam_init):
    lp = lp_ref[...]
    lam = (jnp.exp(jnp.sum(lp[0:1] * lp[1:2], axis=-1, keepdims=True))
           - jnp.exp(jnp.sum(lp[2:3] * lp[3:4], axis=-1, keepdims=True)) + lam_init)
    q = q_ref[0, 0]
    k = k_ref[0, 0]
    nt = (((1,), (1,)), ((), ()))

    def softmax_parts(m):
        s = lax.dot_general(q[:, m * qk_dim:(m + 1) * qk_dim], k[:, m * qk_dim:(m + 1) * qk_dim], nt,
                            preferred_element_type=F32)
        e = jnp.exp(s - jnp.max(s, axis=-1, keepdims=True))
        return e, 1.0 / jnp.sum(e, axis=-1, keepdims=True)

    e0, r0 = softmax_parts(0)
    e1, r1 = softmax_parts(1)
    w = e0 * r0 - e1 * (lam * r1)
    o = _dot(w.astype(BF16), v_ref[0, 0])
    o = o * lax.rsqrt(jnp.mean(o * o, axis=-1, keepdims=True) + EPS)
    o_ref[0] = (o * g_ref[...] * (1.0 - lam_init)).astype(o_ref.dtype)


def _attention(qkv, lam_params, subln_g, n_ctx, qk_dim, lam_init):
    _, bsz, t_all, cols = qkv.shape
    v_dim = subln_g.shape[-1]
    heads = cols // v_dim
    n_lat = t_all - n_ctx
    tq = TOKEN_TILE
    nct = n_ctx // tq
    return pl.pallas_call(
        functools.partial(_attn_kernel, qk_dim=qk_dim, lam_init=lam_init),
        grid=(bsz, heads, n_lat // tq),
        in_specs=[pl.BlockSpec((1, 1, tq, v_dim), lambda b, h, i: (0, b, i + nct, h)),
                  pl.BlockSpec((1, 1, t_all, v_dim), lambda b, h, i: (1, b, 0, h)),
                  pl.BlockSpec((1, 1, t_all, v_dim), lambda b, h, i: (2, b, 0, h)),
                  pl.BlockSpec((4, qk_dim), lambda b, h, i: (0, 0)),
                  pl.BlockSpec((1, v_dim), lambda b, h, i: (0, 0))],
        out_specs=pl.BlockSpec((1, tq, v_dim), lambda b, h, i: (b, i, h)),
        out_shape=jax.ShapeDtypeStruct((bsz, n_lat, cols), BF16),
        compiler_params=_cparams("arbitrary", "arbitrary", "arbitrary"),
        name="diff_attention",
    )(qkv, qkv, qkv, lam_params, subln_g)


def kernel(x, c, ctx, c_ctx, ada_w, ada_b, norm_mix_g, norm_ffn_g, final_g, w_in_e, conv_a_w, conv_b_w, conv_b_b, lru_wa, lru_ba, lru_wi, lru_bi, lru_lam, w_out_e, w_in_o, lam_q1, lam_k1, lam_q2, lam_k2, subln_g, w_out_o, router_w, router_b, exp_w_gate, exp_w_up, exp_w_down):
    bsz, n_lat, d = x.shape
    n_ctx = ctx.shape[1]
    depth = ada_w.shape[0]
    assert depth == 2 and w_in_e.shape[0] == 1 and w_in_o.shape[0] == 1
    assert n_ctx % TOKEN_TILE == 0 and n_lat % TOKEN_TILE == 0 and n_ctx % SCAN_CHUNK == 0
    assert router_w.shape[1] == N_GROUPS * EXPERTS_PER_GROUP
    qk_dim = lam_q1.shape[-1]
    assert qk_dim == LANES

    cond = jnp.concatenate([c, c_ctx[None]], axis=0)
    mod = _ada(cond, ada_w, ada_b).reshape(depth, bsz + 1, 6, d)

    h0 = _prenorm(x, ctx, mod, norm_mix_g, 0)
    za, zb = _mixer0(h0, w_in_e[0], conv_a_w[0], conv_b_w[0], conv_b_b[0], lru_wa[0], lru_ba[0],
                     lru_wi[0], lru_bi[0], lru_lam[0], n_ctx)
    x1, f0, info0, cnt0 = _outproj0(za, zb, w_out_e[0], x, ctx, mod, norm_ffn_g, router_w, router_b, 0)
    o0, pos0 = _moe_sorted_outputs(f0, info0, cnt0, exp_w_gate[0], exp_w_up[0], exp_w_down[0])
    x2, h1 = _combine0(o0, pos0, info0, x1, mod, norm_mix_g, 0, n_ctx)

    lam_init = 0.8 - 0.6 * math.exp(-0.3 * 1)
    qkv = _qkv(h1, w_in_o[0], n_ctx, qk_dim)
    lam_params = jnp.concatenate([lam_q1, lam_k1, lam_q2, lam_k2], axis=0)
    att = _attention(qkv, lam_params, subln_g, n_ctx, qk_dim, lam_init)
    x3, f1, info1, cnt1 = _outproj1(att, w_out_o[0], x2, mod, norm_ffn_g, router_w, router_b, 1, n_ctx)
    o1, pos1 = _moe_sorted_outputs(f1, info1, cnt1, exp_w_gate[1], exp_w_up[1], exp_w_down[1])
    return _combine1(o1, pos1, info1, x3, mod, final_g, 1)
```

```python
import functools
import math

import numpy as np
import jax
import jax.numpy as jnp
from jax import lax
from jax.experimental import pallas as pl
from jax.experimental.pallas import tpu as pltpu

F32 = jnp.float32
BF16 = jnp.bfloat16

EPS = 1e-6
GRID_W = 64
LRU_C = 8.0
ROPE_BASE = 10000.0
N_GROUPS = 4
EXPERTS_PER_GROUP = 4
PAIRS_PER_GROUP = 6
N_CLASSES = N_GROUPS * PAIRS_PER_GROUP
LANES = 128
SUBLANES = 8
TOKEN_TILE = 256
EXPERT_TILE = 256
SCAN_CHUNK = 128
ATTN_Q_TILE = 512
ATTN_Q_SUB = 256
PROJ_TILES = (768, 512, 256)
DMA_ISSUE_UNROLL = 8
VMEM_LIMIT_BYTES = 56 * 1024 * 1024

_PAIRS = [(0, 1), (0, 2), (0, 3), (1, 2), (1, 3), (2, 3)]
_CLASS_LO = np.array([4 * g + i for g in range(N_GROUPS) for (i, j) in _PAIRS], np.int32)
_CLASS_HI = np.array([4 * g + j for g in range(N_GROUPS) for (i, j) in _PAIRS], np.int32)


def _cparams(*sem):
    return pltpu.CompilerParams(dimension_semantics=sem, vmem_limit_bytes=VMEM_LIMIT_BYTES)


def _pick_tile(n, prefs):
    for t in prefs:
        if n % t == 0:
            return t
    raise ValueError(f"no tile in {prefs} divides {n}")


def _sigmoid(v):
    return 1.0 / (1.0 + jnp.exp(-v))


def _dot(a, b):
    return jnp.dot(a, b, preferred_element_type=F32)


def _mod_norm(xv, g, sc, sh):
    ms = jnp.mean(xv * xv, axis=-1, keepdims=True)
    return xv * lax.rsqrt(ms + EPS) * g * (1.0 + sc) + sh


def _ada_kernel(c_ref, w_ref, b_ref, o_ref):
    cv = c_ref[...]
    s = (cv * _sigmoid(cv)).astype(BF16)
    o_ref[0] = _dot(s, w_ref[0].astype(BF16)) + b_ref[0]


def _ada(cond, ada_w, ada_b):
    depth, d, n6 = ada_w.shape
    rows = cond.shape[0]
    tn = min(1024, n6)
    return pl.pallas_call(
        _ada_kernel,
        grid=(depth, n6 // tn),
        in_specs=[pl.BlockSpec((rows, d), lambda l, j: (0, 0)),
                  pl.BlockSpec((1, d, tn), lambda l, j: (l, 0, j)),
                  pl.BlockSpec((1, 1, tn), lambda l, j: (l, 0, j))],
        out_specs=pl.BlockSpec((1, rows, tn), lambda l, j: (l, 0, j)),
        out_shape=jax.ShapeDtypeStruct((depth, rows, n6), F32),
        compiler_params=_cparams("arbitrary", "arbitrary"),
        name="ada_modulation",
    )(cond, ada_w, ada_b.reshape(depth, 1, n6))


def _prenorm_kernel(x_ref, ctx_ref, mod_ref, g_ref, o_ref, *, n_ctx_tiles):
    i = pl.program_id(1)
    mod = mod_ref[0, 0]

    def run(src_ref):
        o_ref[0] = _mod_norm(src_ref[0], g_ref[...], mod[1:2], mod[0:1]).astype(o_ref.dtype)

    @pl.when(i < n_ctx_tiles)
    def _():
        run(ctx_ref)

    @pl.when(i >= n_ctx_tiles)
    def _():
        run(x_ref)


def _dual_specs(bsz, tm, d, nct):
    x_spec = pl.BlockSpec((1, tm, d), lambda b, i: (b, jnp.maximum(i - nct, 0), 0))
    c_spec = pl.BlockSpec((1, tm, d), lambda b, i: (b, jnp.minimum(i, nct - 1), 0))
    return x_spec, c_spec


def _mod_spec(layer, bsz, nct, d):
    return pl.BlockSpec((1, 1, 6, d), lambda b, i: (layer, jnp.where(i < nct, bsz, b), 0, 0))


def _prenorm(x, ctx, mod, g, layer):
    bsz, n_lat, d = x.shape
    n_ctx = ctx.shape[1]
    tm = TOKEN_TILE
    nct = n_ctx // tm
    t_all = n_ctx + n_lat
    x_spec, c_spec = _dual_specs(bsz, tm, d, nct)
    return pl.pallas_call(
        functools.partial(_prenorm_kernel, n_ctx_tiles=nct),
        grid=(bsz, t_all // tm),
        in_specs=[x_spec, c_spec, _mod_spec(layer, bsz, nct, d),
                  pl.BlockSpec((1, d), lambda b, i: (0, 0))],
        out_specs=pl.BlockSpec((1, tm, d), lambda b, i: (b, i, 0)),
        out_shape=jax.ShapeDtypeStruct((bsz, t_all, d), BF16),
        compiler_params=_cparams("arbitrary", "arbitrary"),
        name="prenorm0",
    )(x, ctx, mod, g[layer:layer + 1])


def _tile_scan(a, b, row, reverse):
    for s in (1, 2, 4):
        if reverse:
            a_s = pltpu.roll(a, SUBLANES - s, 0)
            b_s = pltpu.roll(b, SUBLANES - s, 0)
            ok = row < SUBLANES - s
        else:
            a_s = pltpu.roll(a, s, 0)
            b_s = pltpu.roll(b, s, 0)
            ok = row >= s
        a_s = jnp.where(ok, a_s, 1.0)
        b_s = jnp.where(ok, b_s, 0.0)
        b = a * b_s + b
        a = a * a_s
    return a, b


def _chunk_scan(a_ref, b_ref, r0, hcar, row8, reverse):
    n_tiles = SCAN_CHUNK // SUBLANES
    a = a_ref[pl.ds(r0, SCAN_CHUNK), :]
    b = b_ref[pl.ds(r0, SCAN_CHUNK), :]
    edge = 0 if reverse else SUBLANES - 1
    order = reversed(range(n_tiles)) if reverse else range(n_tiles)
    for v in order:
        at, bt = _tile_scan(a[v * SUBLANES:(v + 1) * SUBLANES], b[v * SUBLANES:(v + 1) * SUBLANES], row8, reverse)
        b_ref[pl.ds(r0 + v * SUBLANES, SUBLANES), :] = at * hcar + bt
        hcar = (jnp.broadcast_to(at[edge:edge + 1], (SUBLANES, LANES)) * hcar
                + jnp.broadcast_to(bt[edge:edge + 1], (SUBLANES, LANES)))
    return hcar


def _gelu_tanh(v):
    return 0.5 * v * (1.0 + jnp.tanh(math.sqrt(2.0 / math.pi) * (v + 0.044715 * (v * v * v))))


def _mixer_kernel(h_ref, w_ref, gw_ref, p_ref, za_ref, zb_ref, y_s, g_s, af_s, bf_s, ab_s, bb_s,
                  *, t_all, n_ctx, mm_rows):
    ch = SCAN_CHUNK
    n_sub = t_all // ch
    n_mm = t_all // mm_rows
    halo = SUBLANES
    ext = ch + 2 * halo
    zero_halo = jnp.zeros((halo, y_s.shape[1]), F32)
    y_s[pl.ds(0, halo), :] = zero_halo
    y_s[pl.ds(halo + t_all, halo), :] = zero_halo

    prm = p_ref[0]
    ca = [prm[k:k + 1] for k in range(3)]
    cb = [prm[3 + k:4 + k] for k in range(4)]
    cb_bias = prm[7:8]
    dirs = [(prm[8:9], prm[9:10], -LRU_C * jnp.log1p(jnp.exp(-prm[10:11])), af_s, bf_s),
            (prm[11:12], prm[12:13], -LRU_C * jnp.log1p(jnp.exp(-prm[13:14])), ab_s, bb_s)]
    row8 = lax.broadcasted_iota(jnp.int32, (SUBLANES, LANES), 0)

    def mm(c):
        r0 = pl.multiple_of(c * mm_rows, mm_rows)
        y_s[pl.ds(halo + r0, mm_rows), :] = _dot(h_ref[0, pl.ds(r0, mm_rows), :], w_ref[0])

    def shifted(v_ext, back):
        return pltpu.roll(v_ext, back % ext, 0)[halo:halo + ch]

    def stage_a(e):
        r0 = e * ch if isinstance(e, int) else pl.multiple_of(e * ch, ch)
        rows = r0 + lax.broadcasted_iota(jnp.int32, (ch, LANES), 0)
        first = (rows == 0) | (rows == n_ctx)
        second = (rows == 1) | (rows == n_ctx + 1)
        last = (rows == n_ctx - 1) | (rows == t_all - 1)
        prod = y_s[pl.ds(r0, ext), LANES:2 * LANES] * y_s[pl.ds(r0, ext), 2 * LANES:3 * LANES]
        conv = (ca[0] * jnp.where(first, 0.0, shifted(prod, 1))
                + ca[1] * prod[halo:halo + ch]
                + ca[2] * jnp.where(last, 0.0, shifted(prod, -1)))
        za_ref[0, pl.ds(r0, ch), :] = (y_s[pl.ds(halo + r0, ch), 0:LANES] * conv).astype(za_ref.dtype)
        g_s[pl.ds(r0, ch), :] = _gelu_tanh(y_s[pl.ds(halo + r0, ch), 3 * LANES:4 * LANES])
        xb = y_s[pl.ds(r0, ext), 4 * LANES:5 * LANES]
        u = (cb[0] * jnp.where(first | second, 0.0, shifted(xb, 2))
             + cb[1] * jnp.where(first, 0.0, shifted(xb, 1))
             + cb[2] * xb[halo:halo + ch]
             + cb[3] * jnp.where(last, 0.0, shifted(xb, -1))
             + cb_bias)
        gates = _dot(u.astype(BF16), gw_ref[0])
        for k, (ba, bi, cl, a_s, b_s) in enumerate(dirs):
            r = _sigmoid(gates[:, 2 * k * LANES:(2 * k + 1) * LANES] + ba)
            ig = _sigmoid(gates[:, (2 * k + 1) * LANES:(2 * k + 2) * LANES] + bi)
            a = jnp.exp(cl * r)
            a_s[pl.ds(r0, ch), :] = a
            b_s[pl.ds(r0, ch), :] = jnp.sqrt(1.0 - a * a) * (ig * u)

    per = mm_rows // ch

    def warmup(c, carry):
        mm(c)
        return carry

    lax.fori_loop(0, 2, warmup, 0)
    stage_a(0)

    def pipelined(k, carry):
        stage_a(per * k - 3)
        stage_a(per * k - 2)
        mm(k)
        return carry

    lax.fori_loop(2, n_mm, pipelined, 0)

    def tail(e, carry):
        stage_a(e)
        return carry

    lax.fori_loop(n_sub - 3, n_sub, tail, 0)

    n_ctx_chunk = n_ctx // ch

    def scans(k, carry):
        hf, hb = carry
        hf = _chunk_scan(af_s, bf_s, pl.multiple_of(k * ch, ch), hf, row8, False)
        c = jnp.where(k < n_ctx_chunk, n_ctx_chunk - 1 - k, n_sub - 1 - (k - n_ctx_chunk))
        hb = _chunk_scan(ab_s, bb_s, pl.multiple_of(c * ch, ch), hb, row8, True)
        return hf, hb

    zero_state = jnp.zeros((SUBLANES, LANES), F32)
    lax.fori_loop(0, n_sub, scans, (zero_state, zero_state))

    def combine(e, carry):
        r0 = pl.multiple_of(e * ch, ch)
        zb_ref[0, pl.ds(r0, ch), :] = (g_s[pl.ds(r0, ch), :]
                                       * (bf_s[pl.ds(r0, ch), :] + bb_s[pl.ds(r0, ch), :])).astype(zb_ref.dtype)
        return carry

    lax.fori_loop(0, n_sub, combine, 0)


def _mixer0(h, w_in, conv_a_w, conv_b_w, conv_b_b, lru_wa, lru_ba, lru_wi, lru_bi, lru_lam, n_ctx):
    bsz, t_all, d = h.shape
    width = conv_a_w.shape[-1]
    heads = width // LANES
    assert lru_wa.shape[1] == heads and lru_wa.shape[2] == LANES
    mm_rows = 2 * SCAN_CHUNK
    assert t_all % mm_rows == 0 and t_all // mm_rows >= 2
    w_r = w_in.reshape(d, 5, heads, LANES).transpose(2, 0, 1, 3).reshape(heads, d, 5 * LANES).astype(BF16)
    gw = jnp.concatenate([lru_wa[0], lru_wi[0], lru_wa[1], lru_wi[1]], axis=-1).astype(BF16)

    def per_head(v):
        return v.reshape(-1, heads, LANES).transpose(1, 0, 2)

    prm = jnp.concatenate([
        per_head(conv_a_w), per_head(conv_b_w), per_head(conv_b_b[None]),
        per_head(lru_ba[0:1]), per_head(lru_bi[0:1]), per_head(lru_lam[0:1]),
        per_head(lru_ba[1:2]), per_head(lru_bi[1:2]), per_head(lru_lam[1:2]),
        jnp.zeros((heads, 2, LANES), F32)], axis=1)
    out_sds = jax.ShapeDtypeStruct((bsz, t_all, width), BF16)
    out_spec = pl.BlockSpec((1, t_all, LANES), lambda b, j: (b, 0, j))
    seq_scratch = pltpu.VMEM((t_all, LANES), F32)
    return pl.pallas_call(
        functools.partial(_mixer_kernel, t_all=t_all, n_ctx=n_ctx, mm_rows=mm_rows),
        grid=(bsz, heads),
        in_specs=[pl.BlockSpec((1, t_all, d), lambda b, j: (b, 0, 0)),
                  pl.BlockSpec((1, d, 5 * LANES), lambda b, j: (j, 0, 0)),
                  pl.BlockSpec((1, LANES, 4 * LANES), lambda b, j: (j, 0, 0)),
                  pl.BlockSpec((1, 16, LANES), lambda b, j: (j, 0, 0))],
        out_specs=[out_spec, out_spec],
        out_shape=[out_sds, out_sds],
        scratch_shapes=[pltpu.VMEM((t_all + 2 * SUBLANES, 5 * LANES), F32),
                        seq_scratch, seq_scratch, seq_scratch, seq_scratch, seq_scratch],
        compiler_params=_cparams("arbitrary", "arbitrary"),
        name="mixer0_conv_lru",
    )(h, w_r, gw, prm)


def _route(f, rw_ref, rb_ref, carry_ref, info_ref, meta_ref, cnt_ref, tm):
    f_hi = f.astype(BF16)
    f_lo = (f - f_hi.astype(F32)).astype(BF16)
    rw = rw_ref[...]
    rw_hi = rw.astype(BF16)
    rw_lo = (rw - rw_hi.astype(F32)).astype(BF16)
    logits = _dot(f_hi, rw_hi) + _dot(f_lo, rw_hi) + _dot(f_hi, rw_lo)
    lt = logits.T
    n_exp = N_GROUPS * EXPERTS_PER_GROUP
    scores = _sigmoid(lt[0:n_exp])
    sel = scores + rb_ref[...]
    s_rows = [scores[e:e + 1] for e in range(n_exp)]
    v_rows = [sel[e:e + 1] for e in range(n_exp)]
    best = None
    gidx = None
    for g in range(N_GROUPS):
        a, b, c, d = v_rows[4 * g:4 * g + 4]
        hi1, lo1 = jnp.maximum(a, b), jnp.minimum(a, b)
        hi2, lo2 = jnp.maximum(c, d), jnp.minimum(c, d)
        gs = jnp.maximum(hi1, hi2) + jnp.maximum(jnp.minimum(hi1, hi2), jnp.maximum(lo1, lo2))
        if g == 0:
            best, gidx = gs, jnp.zeros(gs.shape, jnp.int32)
        else:
            better = gs > best
            gidx = jnp.where(better, g, gidx)
            best = jnp.where(better, gs, best)
    found = None
    for e in range(n_exp):
        g = e // EXPERTS_PER_GROUP
        rank = jnp.zeros(best.shape, jnp.int32)
        for e2 in range(4 * g, 4 * g + 4):
            if e2 == e:
                continue
            ahead = v_rows[e2] > v_rows[e]
            if e2 < e:
                ahead = ahead | (v_rows[e2] == v_rows[e])
            rank = rank + ahead.astype(jnp.int32)
        chosen = (gidx == g) & (rank < 2)
        if e == 0:
            found = chosen
            lo_idx = jnp.zeros(best.shape, jnp.int32)
            hi_idx = jnp.zeros(best.shape, jnp.int32)
            lo_s = s_rows[0]
            hi_s = s_rows[0]
        else:
            take_lo = chosen & jnp.logical_not(found)
            lo_idx = jnp.where(take_lo, e, lo_idx)
            lo_s = jnp.where(take_lo, s_rows[e], lo_s)
            hi_idx = jnp.where(chosen, e, hi_idx)
            hi_s = jnp.where(chosen, s_rows[e], hi_s)
            found = found | chosen
    denom = lo_s + hi_s
    gate_lo = lo_s / denom
    gate_hi = hi_s / denom
    li = lo_idx - 4 * gidx
    hj = hi_idx - 4 * gidx
    pair = jnp.where(li == 0, 0, jnp.where(li == 1, 3, 5)) + hj - li - 1
    cls = PAIRS_PER_GROUP * gidx + pair
    n_cls_pad = 32
    onehot = (lax.broadcasted_iota(jnp.int32, (n_cls_pad, tm), 0) == cls).astype(F32)
    tri = (lax.broadcasted_iota(jnp.int32, (tm, tm), 0)
           < lax.broadcasted_iota(jnp.int32, (tm, tm), 1)).astype(BF16)
    excl = _dot(onehot.astype(BF16), tri)
    rank_in_cls = jnp.sum(onehot * (excl + carry_ref[:, 0:1]), axis=0, keepdims=True)
    carry_ref[...] = carry_ref[...] + jnp.sum(onehot, axis=1, keepdims=True)
    cnt_ref[...] = carry_ref[...]
    cls_f = cls.astype(F32)
    rid = lax.broadcasted_iota(jnp.int32, (LANES, tm), 0)
    packed = jnp.where(rid == 1, gate_lo, jnp.where(rid == 2, gate_hi, 0.0))
    info_ref[...] = packed.T
    rid8 = lax.broadcasted_iota(jnp.int32, (SUBLANES, tm), 0)
    meta_ref[0] = jnp.where(rid8 == 0, cls_f, jnp.where(rid8 == 1, rank_in_cls, 0.0))


def _post_mixer(xin, y, mod_ref, g_ref, rw_ref, rb_ref, x1_ref, f_ref, info_ref, meta_ref, cnt_ref, carry_ref, tm):
    first = (pl.program_id(0) == 0) & (pl.program_id(1) == 0)

    @pl.when(first)
    def _():
        carry_ref[...] = jnp.zeros(carry_ref.shape, F32)

    mod = mod_ref[0, 0]
    x1 = xin + mod[2:3] * y
    x1_ref[0] = x1
    f = _mod_norm(x1, g_ref[...], mod[4:5], mod[3:4])
    f_ref[0] = f
    _route(f, rw_ref, rb_ref, carry_ref, info_ref, meta_ref, cnt_ref, tm)


def _outproj0_kernel(za_ref, zb_ref, wa_ref, wb_ref, x_ref, ctx_ref, mod_ref, g_ref, rw_ref, rb_ref,
                     x1_ref, f_ref, info_ref, meta_ref, cnt_ref, carry_ref, *, n_ctx_tiles, tm):
    y = _dot(za_ref[0], wa_ref[...]) + _dot(zb_ref[0], wb_ref[...])
    i = pl.program_id(1)
    xin = jnp.where(i < n_ctx_tiles, ctx_ref[0], x_ref[0])
    _post_mixer(xin, y, mod_ref, g_ref, rw_ref, rb_ref, x1_ref, f_ref, info_ref, meta_ref, cnt_ref, carry_ref, tm)


def _outproj1_kernel(z_ref, w_ref, x_ref, mod_ref, g_ref, rw_ref, rb_ref,
                     x1_ref, f_ref, info_ref, meta_ref, cnt_ref, carry_ref, *, tm):
    y = _dot(z_ref[0], w_ref[...])
    _post_mixer(x_ref[0], y, mod_ref, g_ref, rw_ref, rb_ref, x1_ref, f_ref, info_ref, meta_ref, cnt_ref, carry_ref, tm)


def _router_operands(router_w, router_b):
    d, n_exp = router_w.shape
    rw = jnp.zeros((d, LANES), F32).at[:, :n_exp].set(router_w)
    return rw, router_b.reshape(n_exp, 1)


def _post_outs(bsz, tiles, tm, d):
    n_tok = bsz * tiles * tm
    out_specs = [pl.BlockSpec((1, tm, d), lambda b, i: (b, i, 0)),
                 pl.BlockSpec((1, tm, d), lambda b, i: (b, i, 0)),
                 pl.BlockSpec((tm, LANES), lambda b, i: (b * tiles + i, 0)),
                 pl.BlockSpec((1, SUBLANES, tm), lambda b, i: (b * tiles + i, 0, 0)),
                 pl.BlockSpec((32, LANES), lambda b, i: (0, 0))]
    out_shape = [jax.ShapeDtypeStruct((bsz, tiles * tm, d), F32),
                 jax.ShapeDtypeStruct((bsz, tiles * tm, d), F32),
                 jax.ShapeDtypeStruct((n_tok, LANES), F32),
                 jax.ShapeDtypeStruct((bsz * tiles, SUBLANES, tm), F32),
                 jax.ShapeDtypeStruct((32, LANES), F32)]
    return out_specs, out_shape


def _outproj0(za, zb, w_out, x, ctx, mod, g_ffn, router_w, router_b, layer):
    bsz, t_all, width = za.shape
    d = x.shape[-1]
    tm = TOKEN_TILE
    nct = ctx.shape[1] // tm
    tiles = t_all // tm
    w_bf = w_out.astype(BF16)
    rw, rb = _router_operands(router_w, router_b)
    x_spec, c_spec = _dual_specs(bsz, tm, d, nct)
    z_spec = pl.BlockSpec((1, tm, width), lambda b, i: (b, i, 0))
    out_specs, out_shape = _post_outs(bsz, tiles, tm, d)
    return pl.pallas_call(
        functools.partial(_outproj0_kernel, n_ctx_tiles=nct, tm=tm),
        grid=(bsz, tiles),
        in_specs=[z_spec, z_spec,
                  pl.BlockSpec((width, d), lambda b, i: (0, 0)),
                  pl.BlockSpec((width, d), lambda b, i: (1, 0)),
                  x_spec, c_spec, _mod_spec(layer, bsz, nct, d),
                  pl.BlockSpec((1, d), lambda b, i: (0, 0)),
                  pl.BlockSpec((d, LANES), lambda b, i: (0, 0)),
                  pl.BlockSpec(rb.shape, lambda b, i: (0, 0))],
        out_specs=out_specs,
        out_shape=out_shape,
        scratch_shapes=[pltpu.VMEM((32, LANES), F32)],
        compiler_params=_cparams("arbitrary", "arbitrary"),
        name="outproj0_router",
    )(za, zb, w_bf, w_bf, x, ctx, mod, g_ffn[layer:layer + 1], rw, rb)


def _outproj1(z, w_out, x_all, mod, g_ffn, router_w, router_b, layer, n_ctx):
    bsz, n_lat, width = z.shape
    d = x_all.shape[-1]
    tm = TOKEN_TILE
    nct = n_ctx // tm
    tiles = n_lat // tm
    rw, rb = _router_operands(router_w, router_b)
    out_specs, out_shape = _post_outs(bsz, tiles, tm, d)
    return pl.pallas_call(
        functools.partial(_outproj1_kernel, tm=tm),
        grid=(bsz, tiles),
        in_specs=[pl.BlockSpec((1, tm, width), lambda b, i: (b, i, 0)),
                  pl.BlockSpec((width, d), lambda b, i: (0, 0)),
                  pl.BlockSpec((1, tm, d), lambda b, i: (b, i + nct, 0)),
                  pl.BlockSpec((1, 1, 6, d), lambda b, i: (layer, b, 0, 0)),
                  pl.BlockSpec((1, d), lambda b, i: (0, 0)),
                  pl.BlockSpec((d, LANES), lambda b, i: (0, 0)),
                  pl.BlockSpec(rb.shape, lambda b, i: (0, 0))],
        out_specs=out_specs,
        out_shape=out_shape,
        scratch_shapes=[pltpu.VMEM((32, LANES), F32)],
        compiler_params=_cparams("arbitrary", "arbitrary"),
        name="outproj1_router",
    )(z, w_out.astype(BF16), x_all, mod, g_ffn[layer:layer + 1], rw, rb)


def _dispatch_plan(meta, counts, n_tok, tme):
    cls = meta[:, 0, :].reshape(-1).astype(jnp.int32)
    rank = meta[:, 1, :].reshape(-1).astype(jnp.int32)
    cnt = counts[:N_CLASSES, 0].astype(jnp.int32)
    padded = ((cnt + tme - 1) // tme) * tme
    ends = jnp.cumsum(padded)
    starts = ends - padded
    pos = starts[cls] + rank
    r_pad = n_tok + N_CLASSES * tme
    src = jnp.zeros((r_pad,), jnp.int32).at[pos].set(jnp.arange(n_tok, dtype=jnp.int32))
    n_tiles = r_pad // tme
    cls_tiles = padded // tme
    cls_tile_start = starts // tme
    seg_cls = np.array([[c for c in range(N_CLASSES) if _CLASS_LO[c] == e or _CLASS_HI[c] == e]
                        for e in range(N_GROUPS * EXPERTS_PER_GROUP)], np.int32).reshape(-1)
    seg_exp = np.repeat(np.arange(N_GROUPS * EXPERTS_PER_GROUP, dtype=np.int32), 3)
    seg_slot = (_CLASS_HI[seg_cls] == seg_exp).astype(np.int32)
    seg_len = cls_tiles[seg_cls]
    seg_end = jnp.cumsum(seg_len)
    seg_start = seg_end - seg_len
    total = seg_end[-1]
    n_steps = 2 * n_tiles
    q = jnp.arange(n_steps, dtype=jnp.int32)
    qc = jnp.minimum(q, total - 1)
    seg = jnp.sum((seg_end[None, :] <= qc[:, None]).astype(jnp.int32), axis=1)
    real = q < total
    spare = q - total
    step_tile = jnp.where(real, cls_tile_start[seg_cls][seg] + (qc - seg_start[seg]), total // 2 + spare // 2)
    step_exp = jnp.asarray(seg_exp)[seg]
    step_slot = jnp.where(real, jnp.asarray(seg_slot)[seg], spare % 2)
    step_valid = real.astype(jnp.int32)
    return pos, src, r_pad, (step_exp, step_tile, step_slot, step_valid)


def _gather_rows(idx_ref, nxt_ref, src_hbm, buf, sem, rows, step, n_steps):
    def issue(ref, slot):
        def body(r, carry):
            pltpu.make_async_copy(src_hbm.at[pl.ds(ref[0, 0, r], 1), :], buf.at[slot, pl.ds(r, 1), :],
                                  sem.at[slot]).start()
            return carry
        lax.fori_loop(0, rows, body, 0, unroll=DMA_ISSUE_UNROLL)

    slot = step % 2

    @pl.when(step == 0)
    def _():
        issue(idx_ref, 0)

    @pl.when(step + 1 < n_steps)
    def _():
        issue(nxt_ref, 1 - slot)

    pltpu.make_async_copy(src_hbm.at[pl.ds(0, rows), :], buf.at[slot], sem.at[slot]).wait()
    return slot


def _gather_sorted_kernel(idx_ref, nxt_ref, src_hbm, o_ref, buf, sem, *, rows, n_steps):
    slot = _gather_rows(idx_ref, nxt_ref, src_hbm, buf, sem, rows, pl.program_id(0), n_steps)
    o_ref[...] = buf[slot].astype(o_ref.dtype)


def _gather_sorted(f_flat, src, tme):
    n_tok, d = f_flat.shape
    r_pad = src.shape[0]
    n_tiles = r_pad // tme
    idx = src.reshape(n_tiles, 1, tme)
    return pl.pallas_call(
        functools.partial(_gather_sorted_kernel, rows=tme, n_steps=n_tiles),
        grid=(n_tiles,),
        in_specs=[pl.BlockSpec((1, 1, tme), lambda i: (i, 0, 0), memory_space=pltpu.SMEM),
                  pl.BlockSpec((1, 1, tme), lambda i: (jnp.minimum(i + 1, n_tiles - 1), 0, 0),
                               memory_space=pltpu.SMEM),
                  pl.BlockSpec(memory_space=pl.ANY)],
        out_specs=pl.BlockSpec((tme, d), lambda i: (i, 0)),
        out_shape=jax.ShapeDtypeStruct((r_pad, d), BF16),
        scratch_shapes=[pltpu.VMEM((2, tme, d), F32), pltpu.SemaphoreType.DMA((2,))],
        compiler_params=_cparams("arbitrary"),
        name="moe_gather_sorted",
    )(idx, idx, f_flat)


def _expert_kernel(exp_ref, tile_ref, slot_ref, valid_ref, x_ref, wg_ref, wu_ref, wd_ref, o_ref):
    s = pl.program_id(0)

    @pl.when(valid_ref[s] == 1)
    def _():
        xv = x_ref[...]
        hg = _dot(xv, wg_ref[0])
        hu = _dot(xv, wu_ref[0])
        hid = (hg * _sigmoid(hg) * hu).astype(BF16)
        o_ref[...] = _dot(hid, wd_ref[0])

    @pl.when(valid_ref[s] == 0)
    def _():
        o_ref[...] = jnp.zeros(o_ref.shape, o_ref.dtype)


def _experts(xs, plan, w_gate, w_up, w_down, tme):
    r_pad, d = xs.shape
    n_exp, _, ff = w_gate.shape
    step_exp, step_tile, step_slot, step_valid = plan
    n_steps = step_exp.shape[0]
    grid_spec = pltpu.PrefetchScalarGridSpec(
        num_scalar_prefetch=4,
        grid=(n_steps,),
        in_specs=[pl.BlockSpec((tme, d), lambda s, e, t, w, v: (t[s], 0)),
                  pl.BlockSpec((1, d, ff), lambda s, e, t, w, v: (e[s], 0, 0)),
                  pl.BlockSpec((1, d, ff), lambda s, e, t, w, v: (e[s], 0, 0)),
                  pl.BlockSpec((1, ff, d), lambda s, e, t, w, v: (e[s], 0, 0))],
        out_specs=pl.BlockSpec((tme, d), lambda s, e, t, w, v: (t[s], w[s])),
    )
    return pl.pallas_call(
        _expert_kernel,
        grid_spec=grid_spec,
        out_shape=jax.ShapeDtypeStruct((r_pad, 2 * d), F32),
        compiler_params=_cparams("arbitrary"),
        name="moe_experts",
    )(step_exp, step_tile, step_slot, step_valid, xs,
      w_gate.astype(BF16), w_up.astype(BF16), w_down.astype(BF16))


def _moe_residual(rows, info_ref, x1_ref, mod_ref, d):
    info = info_ref[...]
    y = info[:, 1:2] * rows[:, 0:d] + info[:, 2:3] * rows[:, d:2 * d]
    return x1_ref[0] + mod_ref[0, 0][5:6] * y


def _linear_step(tiles):
    return pl.program_id(0) * tiles + pl.program_id(1)


def _combine0_kernel(idx_ref, nxt_ref, o_hbm, info_ref, x1_ref, mod_ref, modn_ref, g_ref, x2_ref, h_ref,
                     buf, sem, *, rows, d, tiles, n_steps):
    slot = _gather_rows(idx_ref, nxt_ref, o_hbm, buf, sem, rows, _linear_step(tiles), n_steps)
    x2 = _moe_residual(buf[slot], info_ref, x1_ref, mod_ref, d)
    x2_ref[0] = x2
    modn = modn_ref[0, 0]
    h_ref[0] = _mod_norm(x2, g_ref[...], modn[1:2], modn[0:1]).astype(h_ref.dtype)


def _combine1_kernel(idx_ref, nxt_ref, o_hbm, info_ref, x1_ref, mod_ref, g_ref, out_ref, buf, sem,
                     *, rows, d, tiles, n_steps):
    slot = _gather_rows(idx_ref, nxt_ref, o_hbm, buf, sem, rows, _linear_step(tiles), n_steps)
    x2 = _moe_residual(buf[slot], info_ref, x1_ref, mod_ref, d)
    ms = jnp.mean(x2 * x2, axis=-1, keepdims=True)
    out_ref[0] = x2 * lax.rsqrt(ms + EPS) * g_ref[...]


def _pos_specs(tiles, n_steps, tm):
    cur = pl.BlockSpec((1, 1, tm), lambda b, i: (b * tiles + i, 0, 0), memory_space=pltpu.SMEM)
    nxt = pl.BlockSpec((1, 1, tm), lambda b, i: (jnp.minimum(b * tiles + i + 1, n_steps - 1), 0, 0),
                       memory_space=pltpu.SMEM)
    return cur, nxt


def _combine0(o_sorted, pos, info, x1, mod, norm_mix_g, layer, n_ctx):
    bsz, t_all, d = x1.shape
    tm = TOKEN_TILE
    tiles = t_all // tm
    nct = n_ctx // tm
    n_steps = bsz * tiles
    tok_spec = pl.BlockSpec((1, tm, d), lambda b, i: (b, i, 0))
    cur, nxt = _pos_specs(tiles, n_steps, tm)
    idx = pos.reshape(n_steps, 1, tm)
    return pl.pallas_call(
        functools.partial(_combine0_kernel, rows=tm, d=d, tiles=tiles, n_steps=n_steps),
        grid=(bsz, tiles),
        in_specs=[cur, nxt,
                  pl.BlockSpec(memory_space=pl.ANY),
                  pl.BlockSpec((tm, LANES), lambda b, i: (b * tiles + i, 0)),
                  tok_spec,
                  _mod_spec(layer, bsz, nct, d),
                  _mod_spec(layer + 1, bsz, nct, d),
                  pl.BlockSpec((1, d), lambda b, i: (0, 0))],
        out_specs=[tok_spec, tok_spec],
        out_shape=[jax.ShapeDtypeStruct((bsz, t_all, d), F32), jax.ShapeDtypeStruct((bsz, t_all, d), BF16)],
        scratch_shapes=[pltpu.VMEM((2, tm, 2 * d), F32), pltpu.SemaphoreType.DMA((2,))],
        compiler_params=_cparams("arbitrary", "arbitrary"),
        name="moe_combine0",
    )(idx, idx, o_sorted, info, x1, mod, mod, norm_mix_g[layer + 1:layer + 2])


def _combine1(o_sorted, pos, info, x1, mod, final_g, layer):
    bsz, n_lat, d = x1.shape
    tm = TOKEN_TILE
    tiles = n_lat // tm
    n_steps = bsz * tiles
    tok_spec = pl.BlockSpec((1, tm, d), lambda b, i: (b, i, 0))
    cur, nxt = _pos_specs(tiles, n_steps, tm)
    idx = pos.reshape(n_steps, 1, tm)
    return pl.pallas_call(
        functools.partial(_combine1_kernel, rows=tm, d=d, tiles=tiles, n_steps=n_steps),
        grid=(bsz, tiles),
        in_specs=[cur, nxt,
                  pl.BlockSpec(memory_space=pl.ANY),
                  pl.BlockSpec((tm, LANES), lambda b, i: (b * tiles + i, 0)),
                  tok_spec,
                  pl.BlockSpec((1, 1, 6, d), lambda b, i: (layer, b, 0, 0)),
                  pl.BlockSpec((1, d), lambda b, i: (0, 0))],
        out_specs=tok_spec,
        out_shape=jax.ShapeDtypeStruct((bsz, n_lat, d), F32),
        scratch_shapes=[pltpu.VMEM((2, tm, 2 * d), F32), pltpu.SemaphoreType.DMA((2,))],
        compiler_params=_cparams("arbitrary", "arbitrary"),
        name="moe_combine1",
    )(idx, idx, o_sorted, info, x1, mod, final_g.reshape(1, d))


def _moe_sorted_outputs(f, meta, counts, w_gate, w_up, w_down):
    d = f.shape[-1]
    f_flat = f.reshape(-1, d)
    n_tok = f_flat.shape[0]
    tme = EXPERT_TILE
    pos, src, _, plan = _dispatch_plan(meta, counts, n_tok, tme)
    xs = _gather_sorted(f_flat, src, tme)
    return _experts(xs, plan, w_gate, w_up, w_down, tme), pos


def _rope_tables(n_ctx, n_lat, qk_dim):
    axis_dim = qk_dim // 2
    half = axis_dim // 2
    rows = n_lat // GRID_W
    row = jnp.repeat(jnp.arange(rows, dtype=F32), GRID_W)
    col = jnp.tile(jnp.arange(GRID_W, dtype=F32), rows)
    inv = ROPE_BASE ** (-jnp.arange(0, axis_dim, 2, dtype=F32) / axis_dim)
    ang_r = row[:, None] * inv
    ang_c = col[:, None] * inv
    ang = jnp.concatenate([ang_r, ang_r, ang_c, ang_c], axis=-1)
    cos, sin = jnp.cos(ang), jnp.sin(ang)
    lower = (jnp.arange(qk_dim) % axis_dim) < half
    sin_up = jnp.where(lower, -sin, 0.0)
    sin_dn = jnp.where(lower, 0.0, sin)
    pad = jnp.zeros((n_ctx, qk_dim), F32)
    return (jnp.concatenate([pad + 1.0, cos], axis=0), jnp.concatenate([pad, sin_up], axis=0),
            jnp.concatenate([pad, sin_dn], axis=0))


def _proj_kernel(h_ref, w_ref, cos_ref, su_ref, sd_ref, o_ref, *, qk_dim, mult, rope_axis):
    out = o_ref.at[(0,) * (len(o_ref.shape) - 2)]
    y = _dot(h_ref[0], w_ref[...])

    def store_rope():
        half = qk_dim // 4
        cos, su, sd = cos_ref[...], su_ref[...], sd_ref[...]
        for k in range(y.shape[1] // qk_dim):
            t = y[:, k * qk_dim:(k + 1) * qk_dim]
            up = pltpu.roll(t, qk_dim - half, 1)
            dn = pltpu.roll(t, half, 1)
            r = t * cos + up * su + dn * sd
            if mult != 1.0:
                r = r * mult
            out[:, k * qk_dim:(k + 1) * qk_dim] = r.astype(out.dtype)

    if rope_axis is None:
        store_rope()
    else:
        j = pl.program_id(rope_axis)

        @pl.when(j == 0)
        def _():
            store_rope()

        @pl.when(j != 0)
        def _():
            out[...] = y.astype(out.dtype)


def _qkv(h, w_in_o, n_ctx, qk_dim):
    bsz, t_all, d = h.shape
    cols = w_in_o.shape[-1] // 3
    n_lat = t_all - n_ctx
    w_bf = w_in_o.astype(BF16)
    cos, su, sd = _rope_tables(n_ctx, n_lat, qk_dim)
    tq = _pick_tile(n_lat, PROJ_TILES)
    assert n_ctx % tq == 0 or tq % n_ctx == 0
    q_scale = qk_dim ** -0.5 * math.log2(math.e)
    if n_ctx % tq == 0:
        off = n_ctx // tq
        h_spec = pl.BlockSpec((1, tq, d), lambda b, i: (b, i + off, 0))
        h_q = h
    else:
        h_q = h[:, n_ctx:]
        h_spec = pl.BlockSpec((1, tq, d), lambda b, i: (b, i, 0))
    tab_q = pl.BlockSpec((tq, qk_dim), lambda b, i: (i, 0))
    q = pl.pallas_call(
        functools.partial(_proj_kernel, qk_dim=qk_dim, mult=q_scale, rope_axis=None),
        grid=(bsz, n_lat // tq),
        in_specs=[h_spec, pl.BlockSpec((d, cols), lambda b, i: (0, 0)), tab_q, tab_q, tab_q],
        out_specs=pl.BlockSpec((1, tq, cols), lambda b, i: (b, i, 0)),
        out_shape=jax.ShapeDtypeStruct((bsz, n_lat, cols), BF16),
        compiler_params=_cparams("arbitrary", "arbitrary"),
        name="q_rope",
    )(h_q, w_bf, cos[n_ctx:], su[n_ctx:], sd[n_ctx:])
    tk = _pick_tile(t_all, PROJ_TILES)
    tab_k = pl.BlockSpec((tk, qk_dim), lambda j, b, i: (i, 0))
    kv = pl.pallas_call(
        functools.partial(_proj_kernel, qk_dim=qk_dim, mult=1.0, rope_axis=0),
        grid=(2, bsz, t_all // tk),
        in_specs=[pl.BlockSpec((1, tk, d), lambda j, b, i: (b, i, 0)),
                  pl.BlockSpec((d, cols), lambda j, b, i: (0, j + 1)), tab_k, tab_k, tab_k],
        out_specs=pl.BlockSpec((1, 1, tk, cols), lambda j, b, i: (j, b, i, 0)),
        out_shape=jax.ShapeDtypeStruct((2, bsz, t_all, cols), BF16),
        compiler_params=_cparams("arbitrary", "arbitrary", "arbitrary"),
        name="kv_rope",
    )(h, w_bf, cos, su, sd)
    return q, kv


def _attn_kernel(q_ref, k_ref, v_ref, lp_ref, g_ref, o_ref, *, qk_dim, lam_init, sub):
    lp = lp_ref[...]
    lam = (jnp.exp(jnp.sum(lp[0:1] * lp[1:2], axis=-1, keepdims=True))
           - jnp.exp(jnp.sum(lp[2:3] * lp[3:4], axis=-1, keepdims=True)) + lam_init)
    k = k_ref[0, 0]
    v = v_ref[0, 0]
    nt = (((1,), (1,)), ((), ()))
    for t in range(q_ref.shape[1] // sub):
        q = q_ref[0, t * sub:(t + 1) * sub, :]
        parts = []
        for m in range(2):
            s = lax.dot_general(q[:, m * qk_dim:(m + 1) * qk_dim], k[:, m * qk_dim:(m + 1) * qk_dim], nt,
                                preferred_element_type=F32)
            e = jnp.exp2(s - jnp.max(s, axis=-1, keepdims=True))
            parts.append((e, 1.0 / jnp.sum(e, axis=-1, keepdims=True)))
        (e0, r0), (e1, r1) = parts
        w = e0 * r0 - e1 * (lam * r1)
        o = _dot(w.astype(BF16), v)
        o = o * lax.rsqrt(jnp.mean(o * o, axis=-1, keepdims=True) + EPS)
        o_ref[0, t * sub:(t + 1) * sub, :] = (o * g_ref[...] * (1.0 - lam_init)).astype(o_ref.dtype)


def _attention(q, kv, lam_params, subln_g, qk_dim, lam_init):
    bsz, n_lat, cols = q.shape
    t_all = kv.shape[2]
    v_dim = subln_g.shape[-1]
    heads = cols // v_dim
    tq = _pick_tile(n_lat, (ATTN_Q_TILE, ATTN_Q_SUB))
    return pl.pallas_call(
        functools.partial(_attn_kernel, qk_dim=qk_dim, lam_init=lam_init, sub=ATTN_Q_SUB),
        grid=(bsz, heads, n_lat // tq),
        in_specs=[pl.BlockSpec((1, tq, v_dim), lambda b, h, i: (b, i, h)),
                  pl.BlockSpec((1, 1, t_all, v_dim), lambda b, h, i: (0, b, 0, h)),
                  pl.BlockSpec((1, 1, t_all, v_dim), lambda b, h, i: (1, b, 0, h)),
                  pl.BlockSpec((4, qk_dim), lambda b, h, i: (0, 0)),
                  pl.BlockSpec((1, v_dim), lambda b, h, i: (0, 0))],
        out_specs=pl.BlockSpec((1, tq, v_dim), lambda b, h, i: (b, i, h)),
        out_shape=jax.ShapeDtypeStruct((bsz, n_lat, cols), BF16),
        compiler_params=_cparams("arbitrary", "arbitrary", "arbitrary"),
        name="diff_attention",
    )(q, kv, kv, lam_params, subln_g)


def kernel(x, c, ctx, c_ctx, ada_w, ada_b, norm_mix_g, norm_ffn_g, final_g, w_in_e, conv_a_w, conv_b_w, conv_b_b, lru_wa, lru_ba, lru_wi, lru_bi, lru_lam, w_out_e, w_in_o, lam_q1, lam_k1, lam_q2, lam_k2, subln_g, w_out_o, router_w, router_b, exp_w_gate, exp_w_up, exp_w_down):
    bsz, n_lat, d = x.shape
    n_ctx = ctx.shape[1]
    depth = ada_w.shape[0]
    assert depth == 2 and w_in_e.shape[0] == 1 and w_in_o.shape[0] == 1
    assert n_ctx % TOKEN_TILE == 0 and n_lat % TOKEN_TILE == 0 and n_ctx % SCAN_CHUNK == 0
    assert router_w.shape[1] == N_GROUPS * EXPERTS_PER_GROUP
    qk_dim = lam_q1.shape[-1]
    assert qk_dim == LANES

    cond = jnp.concatenate([c, c_ctx[None]], axis=0)
    mod = _ada(cond, ada_w, ada_b).reshape(depth, bsz + 1, 6, d)

    h0 = _prenorm(x, ctx, mod, norm_mix_g, 0)
    za, zb = _mixer0(h0, w_in_e[0], conv_a_w[0], conv_b_w[0], conv_b_b[0], lru_wa[0], lru_ba[0],
                     lru_wi[0], lru_bi[0], lru_lam[0], n_ctx)
    x1, f0, info0, meta0, cnt0 = _outproj0(za, zb, w_out_e[0], x, ctx, mod, norm_ffn_g, router_w, router_b, 0)
    o0, pos0 = _moe_sorted_outputs(f0, meta0, cnt0, exp_w_gate[0], exp_w_up[0], exp_w_down[0])
    x2, h1 = _combine0(o0, pos0, info0, x1, mod, norm_mix_g, 0, n_ctx)

    lam_init = 0.8 - 0.6 * math.exp(-0.3 * 1)
    q, kv = _qkv(h1, w_in_o[0], n_ctx, qk_dim)
    lam_params = jnp.concatenate([lam_q1, lam_k1, lam_q2, lam_k2], axis=0)
    att = _attention(q, kv, lam_params, subln_g, qk_dim, lam_init)
    x3, f1, info1, meta1, cnt1 = _outproj1(att, w_out_o[0], x2, mod, norm_ffn_g, router_w, router_b, 1, n_ctx)
    o1, pos1 = _moe_sorted_outputs(f1, meta1, cnt1, exp_w_gate[1], exp_w_up[1], exp_w_down[1])
    return _combine1(o1, pos1, info1, x3, mod, final_g, 1)
```

```python
import functools
import math

import numpy as np
import jax
import jax.numpy as jnp
from jax import lax
from jax.experimental import pallas as pl
from jax.experimental.pallas import tpu as pltpu

F32 = jnp.float32
BF16 = jnp.bfloat16

EPS = 1e-6
GRID_W = 64
LRU_C = 8.0
ROPE_BASE = 10000.0
N_GROUPS = 4
EXPERTS_PER_GROUP = 4
PAIRS_PER_GROUP = 6
N_CLASSES = N_GROUPS * PAIRS_PER_GROUP
LANES = 128
SUBLANES = 8
TOKEN_TILE = 256
EXPERT_TILE = 256
SCAN_CHUNK = 128
ATTN_Q_TILE = 256
ATTN_KEY_CHUNK = 256
PROJ_TILES = (768, 512, 256)
PROJ_COL_GROUP = 512
DMA_ISSUE_UNROLL = 8
VMEM_LIMIT_BYTES = 56 * 1024 * 1024

_PAIRS = [(0, 1), (0, 2), (0, 3), (1, 2), (1, 3), (2, 3)]
_CLASS_LO = np.array([4 * g + i for g in range(N_GROUPS) for (i, j) in _PAIRS], np.int32)
_CLASS_HI = np.array([4 * g + j for g in range(N_GROUPS) for (i, j) in _PAIRS], np.int32)


def _cparams(*sem):
    return pltpu.CompilerParams(dimension_semantics=sem, vmem_limit_bytes=VMEM_LIMIT_BYTES)


def _pick_tile(n, prefs):
    for t in prefs:
        if n % t == 0:
            return t
    raise ValueError(f"no tile in {prefs} divides {n}")


def _sigmoid(v):
    return 1.0 / (1.0 + jnp.exp(-v))


def _dot(a, b):
    return jnp.dot(a, b, preferred_element_type=F32)


def _mod_norm(xv, g, sc, sh):
    ms = jnp.mean(xv * xv, axis=-1, keepdims=True)
    return xv * lax.rsqrt(ms + EPS) * g * (1.0 + sc) + sh


def _ada_kernel(c_ref, w_ref, b_ref, o_ref):
    cv = c_ref[...]
    s = (cv * _sigmoid(cv)).astype(BF16)
    o_ref[0] = _dot(s, w_ref[0].astype(BF16)) + b_ref[0]


def _ada(cond, ada_w, ada_b):
    depth, d, n6 = ada_w.shape
    rows = cond.shape[0]
    tn = min(1024, n6)
    return pl.pallas_call(
        _ada_kernel,
        grid=(depth, n6 // tn),
        in_specs=[pl.BlockSpec((rows, d), lambda l, j: (0, 0)),
                  pl.BlockSpec((1, d, tn), lambda l, j: (l, 0, j)),
                  pl.BlockSpec((1, 1, tn), lambda l, j: (l, 0, j))],
        out_specs=pl.BlockSpec((1, rows, tn), lambda l, j: (l, 0, j)),
        out_shape=jax.ShapeDtypeStruct((depth, rows, n6), F32),
        compiler_params=_cparams("arbitrary", "arbitrary"),
        name="ada_modulation",
    )(cond, ada_w, ada_b.reshape(depth, 1, n6))


def _prenorm_kernel(x_ref, ctx_ref, mod_ref, g_ref, o_ref, *, n_ctx_tiles):
    i = pl.program_id(1)
    mod = mod_ref[0, 0]

    def run(src_ref):
        o_ref[0] = _mod_norm(src_ref[0], g_ref[...], mod[1:2], mod[0:1]).astype(o_ref.dtype)

    @pl.when(i < n_ctx_tiles)
    def _():
        run(ctx_ref)

    @pl.when(i >= n_ctx_tiles)
    def _():
        run(x_ref)


def _dual_specs(bsz, tm, d, nct):
    x_spec = pl.BlockSpec((1, tm, d), lambda b, i: (b, jnp.maximum(i - nct, 0), 0))
    c_spec = pl.BlockSpec((1, tm, d), lambda b, i: (b, jnp.minimum(i, nct - 1), 0))
    return x_spec, c_spec


def _mod_spec(layer, bsz, nct, d):
    return pl.BlockSpec((1, 1, 6, d), lambda b, i: (layer, jnp.where(i < nct, bsz, b), 0, 0))


def _prenorm(x, ctx, mod, g, layer):
    bsz, n_lat, d = x.shape
    n_ctx = ctx.shape[1]
    tm = TOKEN_TILE
    nct = n_ctx // tm
    t_all = n_ctx + n_lat
    x_spec, c_spec = _dual_specs(bsz, tm, d, nct)
    return pl.pallas_call(
        functools.partial(_prenorm_kernel, n_ctx_tiles=nct),
        grid=(bsz, t_all // tm),
        in_specs=[x_spec, c_spec, _mod_spec(layer, bsz, nct, d),
                  pl.BlockSpec((1, d), lambda b, i: (0, 0))],
        out_specs=pl.BlockSpec((1, tm, d), lambda b, i: (b, i, 0)),
        out_shape=jax.ShapeDtypeStruct((bsz, t_all, d), BF16),
        compiler_params=_cparams("arbitrary", "arbitrary"),
        name="prenorm0",
    )(x, ctx, mod, g[layer:layer + 1])


def _tile_scan(a, b, row, reverse):
    for s in (1, 2, 4):
        if reverse:
            a_s = pltpu.roll(a, SUBLANES - s, 0)
            b_s = pltpu.roll(b, SUBLANES - s, 0)
            ok = row < SUBLANES - s
        else:
            a_s = pltpu.roll(a, s, 0)
            b_s = pltpu.roll(b, s, 0)
            ok = row >= s
        a_s = jnp.where(ok, a_s, 1.0)
        b_s = jnp.where(ok, b_s, 0.0)
        b = a * b_s + b
        a = a * a_s
    return a, b


def _chunk_scan(a_ref, b_ref, r0, hcar, row8, reverse):
    n_tiles = SCAN_CHUNK // SUBLANES
    a = a_ref[pl.ds(r0, SCAN_CHUNK), :]
    b = b_ref[pl.ds(r0, SCAN_CHUNK), :]
    edge = 0 if reverse else SUBLANES - 1
    order = reversed(range(n_tiles)) if reverse else range(n_tiles)
    for v in order:
        at, bt = _tile_scan(a[v * SUBLANES:(v + 1) * SUBLANES], b[v * SUBLANES:(v + 1) * SUBLANES], row8, reverse)
        b_ref[pl.ds(r0 + v * SUBLANES, SUBLANES), :] = at * hcar + bt
        hcar = (jnp.broadcast_to(at[edge:edge + 1], (SUBLANES, LANES)) * hcar
                + jnp.broadcast_to(bt[edge:edge + 1], (SUBLANES, LANES)))
    return hcar


def _gelu_tanh(v):
    return 0.5 * v * (1.0 + jnp.tanh(math.sqrt(2.0 / math.pi) * (v + 0.044715 * (v * v * v))))


def _mixer_kernel(h_ref, w_ref, gw_ref, p_ref, za_ref, zb_ref, y_s, g_s, af_s, bf_s, ab_s, bb_s,
                  *, t_all, n_ctx, mm_rows):
    ch = SCAN_CHUNK
    n_sub = t_all // ch
    n_mm = t_all // mm_rows
    halo = SUBLANES
    ext = ch + 2 * halo
    zero_halo = jnp.zeros((halo, y_s.shape[1]), F32)
    y_s[pl.ds(0, halo), :] = zero_halo
    y_s[pl.ds(halo + t_all, halo), :] = zero_halo

    prm = p_ref[0]
    ca = [prm[k:k + 1] for k in range(3)]
    cb = [prm[3 + k:4 + k] for k in range(4)]
    cb_bias = prm[7:8]
    dirs = [(prm[8:9], prm[9:10], -LRU_C * jnp.log1p(jnp.exp(-prm[10:11])), af_s, bf_s),
            (prm[11:12], prm[12:13], -LRU_C * jnp.log1p(jnp.exp(-prm[13:14])), ab_s, bb_s)]
    row8 = lax.broadcasted_iota(jnp.int32, (SUBLANES, LANES), 0)

    def mm(c):
        r0 = pl.multiple_of(c * mm_rows, mm_rows)
        y_s[pl.ds(halo + r0, mm_rows), :] = _dot(h_ref[0, pl.ds(r0, mm_rows), :], w_ref[0])

    def shifted(v_ext, back):
        return pltpu.roll(v_ext, back % ext, 0)[halo:halo + ch]

    def stage_a(e):
        r0 = e * ch if isinstance(e, int) else pl.multiple_of(e * ch, ch)
        rows = r0 + lax.broadcasted_iota(jnp.int32, (ch, LANES), 0)
        first = (rows == 0) | (rows == n_ctx)
        second = (rows == 1) | (rows == n_ctx + 1)
        last = (rows == n_ctx - 1) | (rows == t_all - 1)
        prod = y_s[pl.ds(r0, ext), LANES:2 * LANES] * y_s[pl.ds(r0, ext), 2 * LANES:3 * LANES]
        conv = (ca[0] * jnp.where(first, 0.0, shifted(prod, 1))
                + ca[1] * prod[halo:halo + ch]
                + ca[2] * jnp.where(last, 0.0, shifted(prod, -1)))
        za_ref[0, pl.ds(r0, ch), :] = (y_s[pl.ds(halo + r0, ch), 0:LANES] * conv).astype(za_ref.dtype)
        g_s[pl.ds(r0, ch), :] = _gelu_tanh(y_s[pl.ds(halo + r0, ch), 3 * LANES:4 * LANES])
        xb = y_s[pl.ds(r0, ext), 4 * LANES:5 * LANES]
        u = (cb[0] * jnp.where(first | second, 0.0, shifted(xb, 2))
             + cb[1] * jnp.where(first, 0.0, shifted(xb, 1))
             + cb[2] * xb[halo:halo + ch]
             + cb[3] * jnp.where(last, 0.0, shifted(xb, -1))
             + cb_bias)
        gates = _dot(u.astype(BF16), gw_ref[0])
        for k, (ba, bi, cl, a_s, b_s) in enumerate(dirs):
            r = _sigmoid(gates[:, 2 * k * LANES:(2 * k + 1) * LANES] + ba)
            ig = _sigmoid(gates[:, (2 * k + 1) * LANES:(2 * k + 2) * LANES] + bi)
            a = jnp.exp(cl * r)
            a_s[pl.ds(r0, ch), :] = a
            b_s[pl.ds(r0, ch), :] = jnp.sqrt(1.0 - a * a) * (ig * u)

    per = mm_rows // ch

    def warmup(c, carry):
        mm(c)
        return carry

    lax.fori_loop(0, 2, warmup, 0)
    stage_a(0)

    def pipelined(k, carry):
        stage_a(per * k - 3)
        stage_a(per * k - 2)
        mm(k)
        return carry

    lax.fori_loop(2, n_mm, pipelined, 0)

    def tail(e, carry):
        stage_a(e)
        return carry

    lax.fori_loop(n_sub - 3, n_sub, tail, 0)

    n_ctx_chunk = n_ctx // ch

    def scans(k, carry):
        hf, hb = carry
        hf = _chunk_scan(af_s, bf_s, pl.multiple_of(k * ch, ch), hf, row8, False)
        c = jnp.where(k < n_ctx_chunk, n_ctx_chunk - 1 - k, n_sub - 1 - (k - n_ctx_chunk))
        hb = _chunk_scan(ab_s, bb_s, pl.multiple_of(c * ch, ch), hb, row8, True)
        return hf, hb

    zero_state = jnp.zeros((SUBLANES, LANES), F32)
    lax.fori_loop(0, n_sub, scans, (zero_state, zero_state))

    def combine(e, carry):
        r0 = pl.multiple_of(e * ch, ch)
        zb_ref[0, pl.ds(r0, ch), :] = (g_s[pl.ds(r0, ch), :]
                                       * (bf_s[pl.ds(r0, ch), :] + bb_s[pl.ds(r0, ch), :])).astype(zb_ref.dtype)
        return carry

    lax.fori_loop(0, n_sub, combine, 0)


def _mixer0(h, w_in, conv_a_w, conv_b_w, conv_b_b, lru_wa, lru_ba, lru_wi, lru_bi, lru_lam, n_ctx):
    bsz, t_all, d = h.shape
    width = conv_a_w.shape[-1]
    heads = width // LANES
    assert lru_wa.shape[1] == heads and lru_wa.shape[2] == LANES
    mm_rows = 2 * SCAN_CHUNK
    assert t_all % mm_rows == 0 and t_all // mm_rows >= 2
    w_r = w_in.reshape(d, 5, heads, LANES).transpose(2, 0, 1, 3).reshape(heads, d, 5 * LANES).astype(BF16)
    gw = jnp.concatenate([lru_wa[0], lru_wi[0], lru_wa[1], lru_wi[1]], axis=-1).astype(BF16)

    def per_head(v):
        return v.reshape(-1, heads, LANES).transpose(1, 0, 2)

    prm = jnp.concatenate([
        per_head(conv_a_w), per_head(conv_b_w), per_head(conv_b_b[None]),
        per_head(lru_ba[0:1]), per_head(lru_bi[0:1]), per_head(lru_lam[0:1]),
        per_head(lru_ba[1:2]), per_head(lru_bi[1:2]), per_head(lru_lam[1:2]),
        jnp.zeros((heads, 2, LANES), F32)], axis=1)
    out_sds = jax.ShapeDtypeStruct((bsz, t_all, width), BF16)
    out_spec = pl.BlockSpec((1, t_all, LANES), lambda b, j: (b, 0, j))
    seq_scratch = pltpu.VMEM((t_all, LANES), F32)
    return pl.pallas_call(
        functools.partial(_mixer_kernel, t_all=t_all, n_ctx=n_ctx, mm_rows=mm_rows),
        grid=(bsz, heads),
        in_specs=[pl.BlockSpec((1, t_all, d), lambda b, j: (b, 0, 0)),
                  pl.BlockSpec((1, d, 5 * LANES), lambda b, j: (j, 0, 0)),
                  pl.BlockSpec((1, LANES, 4 * LANES), lambda b, j: (j, 0, 0)),
                  pl.BlockSpec((1, 16, LANES), lambda b, j: (j, 0, 0))],
        out_specs=[out_spec, out_spec],
        out_shape=[out_sds, out_sds],
        scratch_shapes=[pltpu.VMEM((t_all + 2 * SUBLANES, 5 * LANES), F32),
                        seq_scratch, seq_scratch, seq_scratch, seq_scratch, seq_scratch],
        compiler_params=_cparams("arbitrary", "arbitrary"),
        name="mixer0_conv_lru",
    )(h, w_r, gw, prm)


def _split_router_weights(rw_ref, rw2_ref):
    n_exp = N_GROUPS * EXPERTS_PER_GROUP
    rw = rw_ref[...]
    rw_hi = rw.astype(BF16).astype(F32)
    rw2_ref[...] = (rw_hi + pltpu.roll(rw - rw_hi, n_exp, 1)).astype(rw2_ref.dtype)


def _route(f, rw2_ref, rb_ref, carry_ref, info_ref, meta_ref, cnt_ref, tm):
    n_exp = N_GROUPS * EXPERTS_PER_GROUP
    f_hi = f.astype(BF16)
    f_lo = (f - f_hi.astype(F32)).astype(BF16)
    rw2 = rw2_ref[...]
    lt = (_dot(f_hi, rw2) + _dot(f_lo, rw2)).T
    scores = _sigmoid(lt[0:n_exp] + lt[n_exp:2 * n_exp])
    sel = scores + rb_ref[...]
    s_rows = [scores[e:e + 1] for e in range(n_exp)]
    v_rows = [sel[e:e + 1] for e in range(n_exp)]
    best = None
    gidx = None
    for g in range(N_GROUPS):
        a, b, c, d = v_rows[4 * g:4 * g + 4]
        hi1, lo1 = jnp.maximum(a, b), jnp.minimum(a, b)
        hi2, lo2 = jnp.maximum(c, d), jnp.minimum(c, d)
        gs = jnp.maximum(hi1, hi2) + jnp.maximum(jnp.minimum(hi1, hi2), jnp.maximum(lo1, lo2))
        if g == 0:
            best, gidx = gs, jnp.zeros(gs.shape, jnp.int32)
        else:
            better = gs > best
            gidx = jnp.where(better, g, gidx)
            best = jnp.where(better, gs, best)
    found = None
    for e in range(n_exp):
        g = e // EXPERTS_PER_GROUP
        rank = jnp.zeros(best.shape, jnp.int32)
        for e2 in range(4 * g, 4 * g + 4):
            if e2 == e:
                continue
            ahead = v_rows[e2] > v_rows[e]
            if e2 < e:
                ahead = ahead | (v_rows[e2] == v_rows[e])
            rank = rank + ahead.astype(jnp.int32)
        chosen = (gidx == g) & (rank < 2)
        if e == 0:
            found = chosen
            lo_idx = jnp.zeros(best.shape, jnp.int32)
            hi_idx = jnp.zeros(best.shape, jnp.int32)
            lo_s = s_rows[0]
            hi_s = s_rows[0]
        else:
            take_lo = chosen & jnp.logical_not(found)
            lo_idx = jnp.where(take_lo, e, lo_idx)
            lo_s = jnp.where(take_lo, s_rows[e], lo_s)
            hi_idx = jnp.where(chosen, e, hi_idx)
            hi_s = jnp.where(chosen, s_rows[e], hi_s)
            found = found | chosen
    denom = lo_s + hi_s
    gate_lo = lo_s / denom
    gate_hi = hi_s / denom
    li = lo_idx - 4 * gidx
    hj = hi_idx - 4 * gidx
    pair = jnp.where(li == 0, 0, jnp.where(li == 1, 3, 5)) + hj - li - 1
    cls = PAIRS_PER_GROUP * gidx + pair
    n_cls_pad = 32
    onehot = (lax.broadcasted_iota(jnp.int32, (n_cls_pad, tm), 0) == cls).astype(F32)
    tri = (lax.broadcasted_iota(jnp.int32, (tm, tm), 0)
           < lax.broadcasted_iota(jnp.int32, (tm, tm), 1)).astype(BF16)
    excl = _dot(onehot.astype(BF16), tri)
    rank_in_cls = jnp.sum(onehot * (excl + carry_ref[:, 0:1]), axis=0, keepdims=True)
    carry_ref[...] = carry_ref[...] + jnp.sum(onehot, axis=1, keepdims=True)
    cnt_ref[...] = carry_ref[...]
    cls_f = cls.astype(F32)
    rid = lax.broadcasted_iota(jnp.int32, (LANES, tm), 0)
    packed = jnp.where(rid == 1, gate_lo, jnp.where(rid == 2, gate_hi, 0.0))
    info_ref[...] = packed.T
    rid8 = lax.broadcasted_iota(jnp.int32, (SUBLANES, tm), 0)
    meta_ref[0] = jnp.where(rid8 == 0, cls_f, jnp.where(rid8 == 1, rank_in_cls, 0.0))


def _post_mixer(xin_cols, y_cols, mod_ref, g_ref, rw_ref, rb_ref, outs, scratch, tm):
    x1_ref, f_ref, info_ref, meta_ref, cnt_ref = outs
    carry_ref, rw2_ref = scratch
    first = (pl.program_id(0) == 0) & (pl.program_id(1) == 0)

    @pl.when(first)
    def _():
        carry_ref[...] = jnp.zeros(carry_ref.shape, F32)
        _split_router_weights(rw_ref, rw2_ref)

    mod = mod_ref[0, 0]
    d = mod.shape[-1]
    group = min(PROJ_COL_GROUP, d)
    ssq = None
    for g in range(d // group):
        cols = slice(g * group, (g + 1) * group)
        x1g = xin_cols(cols) + mod[2:3, cols] * y_cols(cols)
        x1_ref[0, :, cols] = x1g
        part = jnp.sum(x1g * x1g, axis=-1, keepdims=True)
        ssq = part if ssq is None else ssq + part
    f = (x1_ref[0] * lax.rsqrt(ssq * (1.0 / d) + EPS) * g_ref[...]) * (1.0 + mod[4:5]) + mod[3:4]
    f_ref[0] = f
    _route(f, rw2_ref, rb_ref, carry_ref, info_ref, meta_ref, cnt_ref, tm)


def _outproj0_kernel(za_ref, zb_ref, wa_ref, wb_ref, x_ref, ctx_ref, mod_ref, g_ref, rw_ref, rb_ref,
                     x1_ref, f_ref, info_ref, meta_ref, cnt_ref, carry_ref, rw2_ref, *, n_ctx_tiles, tm):
    is_ctx = pl.program_id(1) < n_ctx_tiles
    za, zb = za_ref[0], zb_ref[0]
    _post_mixer(lambda cols: jnp.where(is_ctx, ctx_ref[0, :, cols], x_ref[0, :, cols]),
                lambda cols: _dot(za, wa_ref[:, cols]) + _dot(zb, wb_ref[:, cols]),
                mod_ref, g_ref, rw_ref, rb_ref, (x1_ref, f_ref, info_ref, meta_ref, cnt_ref),
                (carry_ref, rw2_ref), tm)


def _outproj1_kernel(z_ref, w_ref, x_ref, mod_ref, g_ref, rw_ref, rb_ref,
                     x1_ref, f_ref, info_ref, meta_ref, cnt_ref, carry_ref, rw2_ref, *, tm):
    z = z_ref[0]
    _post_mixer(lambda cols: x_ref[0, :, cols], lambda cols: _dot(z, w_ref[:, cols]),
                mod_ref, g_ref, rw_ref, rb_ref, (x1_ref, f_ref, info_ref, meta_ref, cnt_ref),
                (carry_ref, rw2_ref), tm)


def _router_operands(router_w, router_b):
    d, n_exp = router_w.shape
    rw = jnp.zeros((d, LANES), F32).at[:, :n_exp].set(router_w)
    return rw, router_b.reshape(n_exp, 1)


def _post_outs(bsz, tiles, tm, d):
    n_tok = bsz * tiles * tm
    out_specs = [pl.BlockSpec((1, tm, d), lambda b, i: (b, i, 0)),
                 pl.BlockSpec((1, tm, d), lambda b, i: (b, i, 0)),
                 pl.BlockSpec((tm, LANES), lambda b, i: (b * tiles + i, 0)),
                 pl.BlockSpec((1, SUBLANES, tm), lambda b, i: (b * tiles + i, 0, 0)),
                 pl.BlockSpec((32, LANES), lambda b, i: (0, 0))]
    out_shape = [jax.ShapeDtypeStruct((bsz, tiles * tm, d), F32),
                 jax.ShapeDtypeStruct((bsz, tiles * tm, d), F32),
                 jax.ShapeDtypeStruct((n_tok, LANES), F32),
                 jax.ShapeDtypeStruct((bsz * tiles, SUBLANES, tm), F32),
                 jax.ShapeDtypeStruct((32, LANES), F32)]
    return out_specs, out_shape


def _outproj0(za, zb, w_out, x, ctx, mod, g_ffn, router_w, router_b, layer):
    bsz, t_all, width = za.shape
    d = x.shape[-1]
    tm = TOKEN_TILE
    nct = ctx.shape[1] // tm
    tiles = t_all // tm
    w_bf = w_out.astype(BF16)
    rw, rb = _router_operands(router_w, router_b)
    x_spec, c_spec = _dual_specs(bsz, tm, d, nct)
    z_spec = pl.BlockSpec((1, tm, width), lambda b, i: (b, i, 0))
    out_specs, out_shape = _post_outs(bsz, tiles, tm, d)
    return pl.pallas_call(
        functools.partial(_outproj0_kernel, n_ctx_tiles=nct, tm=tm),
        grid=(bsz, tiles),
        in_specs=[z_spec, z_spec,
                  pl.BlockSpec((width, d), lambda b, i: (0, 0)),
                  pl.BlockSpec((width, d), lambda b, i: (1, 0)),
                  x_spec, c_spec, _mod_spec(layer, bsz, nct, d),
                  pl.BlockSpec((1, d), lambda b, i: (0, 0)),
                  pl.BlockSpec((d, LANES), lambda b, i: (0, 0)),
                  pl.BlockSpec(rb.shape, lambda b, i: (0, 0))],
        out_specs=out_specs,
        out_shape=out_shape,
        scratch_shapes=[pltpu.VMEM((32, LANES), F32), pltpu.VMEM((d, LANES), BF16)],
        compiler_params=_cparams("arbitrary", "arbitrary"),
        name="outproj0_router",
    )(za, zb, w_bf, w_bf, x, ctx, mod, g_ffn[layer:layer + 1], rw, rb)


def _outproj1(z, w_out, x_all, mod, g_ffn, router_w, router_b, layer, n_ctx):
    bsz, n_lat, width = z.shape
    d = x_all.shape[-1]
    tm = TOKEN_TILE
    nct = n_ctx // tm
    tiles = n_lat // tm
    rw, rb = _router_operands(router_w, router_b)
    out_specs, out_shape = _post_outs(bsz, tiles, tm, d)
    return pl.pallas_call(
        functools.partial(_outproj1_kernel, tm=tm),
        grid=(bsz, tiles),
        in_specs=[pl.BlockSpec((1, tm, width), lambda b, i: (b, i, 0)),
                  pl.BlockSpec((width, d), lambda b, i: (0, 0)),
                  pl.BlockSpec((1, tm, d), lambda b, i: (b, i + nct, 0)),
                  pl.BlockSpec((1, 1, 6, d), lambda b, i: (layer, b, 0, 0)),
                  pl.BlockSpec((1, d), lambda b, i: (0, 0)),
                  pl.BlockSpec((d, LANES), lambda b, i: (0, 0)),
                  pl.BlockSpec(rb.shape, lambda b, i: (0, 0))],
        out_specs=out_specs,
        out_shape=out_shape,
        scratch_shapes=[pltpu.VMEM((32, LANES), F32), pltpu.VMEM((d, LANES), BF16)],
        compiler_params=_cparams("arbitrary", "arbitrary"),
        name="outproj1_router",
    )(z, w_out.astype(BF16), x_all, mod, g_ffn[layer:layer + 1], rw, rb)


def _dispatch_plan(meta, counts, n_tok, tme):
    cls = meta[:, 0, :].reshape(-1).astype(jnp.int32)
    rank = meta[:, 1, :].reshape(-1).astype(jnp.int32)
    cnt = counts[:N_CLASSES, 0].astype(jnp.int32)
    padded = ((cnt + tme - 1) // tme) * tme
    ends = jnp.cumsum(padded)
    starts = ends - padded
    pos = starts[cls] + rank
    r_pad = n_tok + N_CLASSES * tme
    src = jnp.zeros((r_pad,), jnp.int32).at[pos].set(jnp.arange(n_tok, dtype=jnp.int32))
    n_tiles = r_pad // tme
    cls_tiles = padded // tme
    cls_tile_start = starts // tme
    seg_cls = np.array([[c for c in range(N_CLASSES) if _CLASS_LO[c] == e or _CLASS_HI[c] == e]
                        for e in range(N_GROUPS * EXPERTS_PER_GROUP)], np.int32).reshape(-1)
    seg_exp = np.repeat(np.arange(N_GROUPS * EXPERTS_PER_GROUP, dtype=np.int32), 3)
    seg_slot = (_CLASS_HI[seg_cls] == seg_exp).astype(np.int32)
    seg_len = cls_tiles[seg_cls]
    seg_end = jnp.cumsum(seg_len)
    seg_start = seg_end - seg_len
    total = seg_end[-1]
    n_steps = 2 * n_tiles
    q = jnp.arange(n_steps, dtype=jnp.int32)
    qc = jnp.minimum(q, total - 1)
    seg = jnp.sum((seg_end[None, :] <= qc[:, None]).astype(jnp.int32), axis=1)
    real = q < total
    spare = q - total
    step_tile = jnp.where(real, cls_tile_start[seg_cls][seg] + (qc - seg_start[seg]), total // 2 + spare // 2)
    step_exp = jnp.asarray(seg_exp)[seg]
    step_slot = jnp.where(real, jnp.asarray(seg_slot)[seg], spare % 2)
    step_valid = real.astype(jnp.int32)
    return pos, src, r_pad, (step_exp, step_tile, step_slot, step_valid)


def _gather_rows(idx_ref, nxt_ref, src_hbm, buf, sem, rows, step, n_steps):
    def issue(ref, slot):
        def body(g, carry):
            for u in range(DMA_ISSUE_UNROLL):
                r = g * DMA_ISSUE_UNROLL + u
                pltpu.make_async_copy(src_hbm.at[pl.ds(ref[0, 0, r], 1), :], buf.at[slot, pl.ds(r, 1), :],
                                      sem.at[slot]).start(priority=u % 2)
            return carry
        lax.fori_loop(0, rows // DMA_ISSUE_UNROLL, body, 0)

    slot = step % 2

    @pl.when(step == 0)
    def _():
        issue(idx_ref, 0)

    @pl.when(step + 1 < n_steps)
    def _():
        issue(nxt_ref, 1 - slot)

    pltpu.make_async_copy(src_hbm.at[pl.ds(0, rows), :], buf.at[slot], sem.at[slot]).wait()
    return slot


def _gather_sorted_kernel(idx_ref, nxt_ref, src_hbm, o_ref, buf, sem, *, rows, n_steps):
    slot = _gather_rows(idx_ref, nxt_ref, src_hbm, buf, sem, rows, pl.program_id(0), n_steps)
    o_ref[...] = buf[slot].astype(o_ref.dtype)


def _gather_sorted(f_flat, src, tme):
    n_tok, d = f_flat.shape
    r_pad = src.shape[0]
    n_tiles = r_pad // tme
    idx = src.reshape(n_tiles, 1, tme)
    return pl.pallas_call(
        functools.partial(_gather_sorted_kernel, rows=tme, n_steps=n_tiles),
        grid=(n_tiles,),
        in_specs=[pl.BlockSpec((1, 1, tme), lambda i: (i, 0, 0), memory_space=pltpu.SMEM),
                  pl.BlockSpec((1, 1, tme), lambda i: (jnp.minimum(i + 1, n_tiles - 1), 0, 0),
                               memory_space=pltpu.SMEM),
                  pl.BlockSpec(memory_space=pl.ANY)],
        out_specs=pl.BlockSpec((tme, d), lambda i: (i, 0)),
        out_shape=jax.ShapeDtypeStruct((r_pad, d), BF16),
        scratch_shapes=[pltpu.VMEM((2, tme, d), F32), pltpu.SemaphoreType.DMA((2,))],
        compiler_params=_cparams("arbitrary"),
        name="moe_gather_sorted",
    )(idx, idx, f_flat)


def _expert_kernel(exp_ref, tile_ref, slot_ref, valid_ref, x_ref, wg_ref, wu_ref, wd_ref, o_ref):
    s = pl.program_id(0)

    @pl.when(valid_ref[s] == 1)
    def _():
        xv = x_ref[...]
        hg = _dot(xv, wg_ref[0, 0])
        hu = _dot(xv, wu_ref[0, 0])
        hid = (hg * _sigmoid(hg) * hu).astype(BF16)
        o_ref[...] = _dot(hid, wd_ref[0, 0])

    @pl.when(valid_ref[s] == 0)
    def _():
        o_ref[...] = jnp.zeros(o_ref.shape, o_ref.dtype)


def _experts(xs, plan, w_gate, w_up, w_down, layer, tme):
    r_pad, d = xs.shape
    ff = w_gate.shape[-1]
    step_exp, step_tile, step_slot, step_valid = plan
    n_steps = step_exp.shape[0]
    grid_spec = pltpu.PrefetchScalarGridSpec(
        num_scalar_prefetch=4,
        grid=(n_steps,),
        in_specs=[pl.BlockSpec((tme, d), lambda s, e, t, w, v: (t[s], 0)),
                  pl.BlockSpec((1, 1, d, ff), lambda s, e, t, w, v: (layer, e[s], 0, 0)),
                  pl.BlockSpec((1, 1, d, ff), lambda s, e, t, w, v: (layer, e[s], 0, 0)),
                  pl.BlockSpec((1, 1, ff, d), lambda s, e, t, w, v: (layer, e[s], 0, 0))],
        out_specs=pl.BlockSpec((tme, d), lambda s, e, t, w, v: (t[s], w[s])),
    )
    return pl.pallas_call(
        _expert_kernel,
        grid_spec=grid_spec,
        out_shape=jax.ShapeDtypeStruct((r_pad, 2 * d), F32),
        compiler_params=_cparams("arbitrary"),
        name="moe_experts",
    )(step_exp, step_tile, step_slot, step_valid, xs, w_gate, w_up, w_down)


def _moe_residual(rows, info_ref, x1_ref, mod_ref, d):
    info = info_ref[...]
    y = info[:, 1:2] * rows[:, 0:d] + info[:, 2:3] * rows[:, d:2 * d]
    return x1_ref[0] + mod_ref[0, 0][5:6] * y


def _linear_step(tiles):
    return pl.program_id(0) * tiles + pl.program_id(1)


def _combine0_kernel(idx_ref, nxt_ref, o_hbm, info_ref, x1_ref, mod_ref, modn_ref, g_ref, x2_ref, h_ref,
                     buf, sem, *, rows, d, tiles, n_steps):
    slot = _gather_rows(idx_ref, nxt_ref, o_hbm, buf, sem, rows, _linear_step(tiles), n_steps)
    x2 = _moe_residual(buf[slot], info_ref, x1_ref, mod_ref, d)
    x2_ref[0] = x2
    modn = modn_ref[0, 0]
    h_ref[0] = _mod_norm(x2, g_ref[...], modn[1:2], modn[0:1]).astype(h_ref.dtype)


def _combine1_kernel(idx_ref, nxt_ref, o_hbm, info_ref, x1_ref, mod_ref, g_ref, out_ref, buf, sem,
                     *, rows, d, tiles, n_steps):
    slot = _gather_rows(idx_ref, nxt_ref, o_hbm, buf, sem, rows, _linear_step(tiles), n_steps)
    x2 = _moe_residual(buf[slot], info_ref, x1_ref, mod_ref, d)
    ms = jnp.mean(x2 * x2, axis=-1, keepdims=True)
    out_ref[0] = x2 * lax.rsqrt(ms + EPS) * g_ref[...]


def _pos_specs(tiles, n_steps, tm):
    cur = pl.BlockSpec((1, 1, tm), lambda b, i: (b * tiles + i, 0, 0), memory_space=pltpu.SMEM)
    nxt = pl.BlockSpec((1, 1, tm), lambda b, i: (jnp.minimum(b * tiles + i + 1, n_steps - 1), 0, 0),
                       memory_space=pltpu.SMEM)
    return cur, nxt


def _combine0(o_sorted, pos, info, x1, mod, norm_mix_g, layer, n_ctx):
    bsz, t_all, d = x1.shape
    tm = TOKEN_TILE
    tiles = t_all // tm
    nct = n_ctx // tm
    n_steps = bsz * tiles
    tok_spec = pl.BlockSpec((1, tm, d), lambda b, i: (b, i, 0))
    cur, nxt = _pos_specs(tiles, n_steps, tm)
    idx = pos.reshape(n_steps, 1, tm)
    return pl.pallas_call(
        functools.partial(_combine0_kernel, rows=tm, d=d, tiles=tiles, n_steps=n_steps),
        grid=(bsz, tiles),
        in_specs=[cur, nxt,
                  pl.BlockSpec(memory_space=pl.ANY),
                  pl.BlockSpec((tm, LANES), lambda b, i: (b * tiles + i, 0)),
                  tok_spec,
                  _mod_spec(layer, bsz, nct, d),
                  _mod_spec(layer + 1, bsz, nct, d),
                  pl.BlockSpec((1, d), lambda b, i: (0, 0))],
        out_specs=[tok_spec, tok_spec],
        out_shape=[jax.ShapeDtypeStruct((bsz, t_all, d), F32), jax.ShapeDtypeStruct((bsz, t_all, d), BF16)],
        scratch_shapes=[pltpu.VMEM((2, tm, 2 * d), F32), pltpu.SemaphoreType.DMA((2,))],
        compiler_params=_cparams("arbitrary", "arbitrary"),
        name="moe_combine0",
    )(idx, idx, o_sorted, info, x1, mod, mod, norm_mix_g[layer + 1:layer + 2])


def _combine1(o_sorted, pos, info, x1, mod, final_g, layer):
    bsz, n_lat, d = x1.shape
    tm = TOKEN_TILE
    tiles = n_lat // tm
    n_steps = bsz * tiles
    tok_spec = pl.BlockSpec((1, tm, d), lambda b, i: (b, i, 0))
    cur, nxt = _pos_specs(tiles, n_steps, tm)
    idx = pos.reshape(n_steps, 1, tm)
    return pl.pallas_call(
        functools.partial(_combine1_kernel, rows=tm, d=d, tiles=tiles, n_steps=n_steps),
        grid=(bsz, tiles),
        in_specs=[cur, nxt,
                  pl.BlockSpec(memory_space=pl.ANY),
                  pl.BlockSpec((tm, LANES), lambda b, i: (b * tiles + i, 0)),
                  tok_spec,
                  pl.BlockSpec((1, 1, 6, d), lambda b, i: (layer, b, 0, 0)),
                  pl.BlockSpec((1, d), lambda b, i: (0, 0))],
        out_specs=tok_spec,
        out_shape=jax.ShapeDtypeStruct((bsz, n_lat, d), F32),
        scratch_shapes=[pltpu.VMEM((2, tm, 2 * d), F32), pltpu.SemaphoreType.DMA((2,))],
        compiler_params=_cparams("arbitrary", "arbitrary"),
        name="moe_combine1",
    )(idx, idx, o_sorted, info, x1, mod, final_g.reshape(1, d))


def _moe_sorted_outputs(f, meta, counts, expert_w, layer):
    d = f.shape[-1]
    f_flat = f.reshape(-1, d)
    n_tok = f_flat.shape[0]
    tme = EXPERT_TILE
    pos, src, _, plan = _dispatch_plan(meta, counts, n_tok, tme)
    xs = _gather_sorted(f_flat, src, tme)
    return _experts(xs, plan, *expert_w, layer, tme), pos


def _rope_tables(n_ctx, n_lat, qk_dim):
    axis_dim = qk_dim // 2
    half = axis_dim // 2
    rows = n_lat // GRID_W
    row = jnp.repeat(jnp.arange(rows, dtype=F32), GRID_W)
    col = jnp.tile(jnp.arange(GRID_W, dtype=F32), rows)
    inv = ROPE_BASE ** (-jnp.arange(0, axis_dim, 2, dtype=F32) / axis_dim)
    ang_r = row[:, None] * inv
    ang_c = col[:, None] * inv
    ang = jnp.concatenate([ang_r, ang_r, ang_c, ang_c], axis=-1)
    cos, sin = jnp.cos(ang), jnp.sin(ang)
    lower = (jnp.arange(qk_dim) % axis_dim) < half
    sin_up = jnp.where(lower, -sin, 0.0)
    sin_dn = jnp.where(lower, 0.0, sin)
    pad = jnp.zeros((n_ctx, qk_dim), F32)
    return (jnp.concatenate([pad + 1.0, cos], axis=0), jnp.concatenate([pad, sin_up], axis=0),
            jnp.concatenate([pad, sin_dn], axis=0))


def _proj_kernel(*refs, qk_dim, mult, rope, group):
    if rope:
        h_ref, w_ref, cos_ref, su_ref, sd_ref, o_ref = refs
        cos, su, sd = cos_ref[...], su_ref[...], sd_ref[...]
        half = qk_dim // 4
    else:
        h_ref, w_ref, o_ref = refs
    h = h_ref[0]
    for g in range(w_ref.shape[1] // group):
        y = _dot(h, w_ref[:, g * group:(g + 1) * group])
        if not rope:
            o_ref[0, :, g * group:(g + 1) * group] = y.astype(o_ref.dtype)
            continue
        for k in range(group // qk_dim):
            t = y[:, k * qk_dim:(k + 1) * qk_dim]
            up = pltpu.roll(t, qk_dim - half, 1)
            dn = pltpu.roll(t, half, 1)
            r = t * cos + up * su + dn * sd
            if mult != 1.0:
                r = r * mult
            c0 = g * group + k * qk_dim
            o_ref[0, :, c0:c0 + qk_dim] = r.astype(o_ref.dtype)


def _projection(h, w_bf, col_block, row_off, n_rows, tables, mult, qk_dim, name):
    bsz, _, d = h.shape
    cols = w_bf.shape[1] // 3
    tm = _pick_tile(math.gcd(n_rows, row_off) if row_off else n_rows, PROJ_TILES)
    off = row_off // tm
    tab = pl.BlockSpec((tm, qk_dim), lambda b, i: (i, 0))
    in_specs = [pl.BlockSpec((1, tm, d), lambda b, i: (b, i + off, 0)),
                pl.BlockSpec((d, cols), lambda b, i: (0, col_block))]
    operands = [h, w_bf]
    if tables is not None:
        in_specs += [tab, tab, tab]
        operands += list(tables)
    return pl.pallas_call(
        functools.partial(_proj_kernel, qk_dim=qk_dim, mult=mult, rope=tables is not None,
                          group=min(PROJ_COL_GROUP, cols)),
        grid=(bsz, n_rows // tm),
        in_specs=in_specs,
        out_specs=pl.BlockSpec((1, tm, cols), lambda b, i: (b, i, 0)),
        out_shape=jax.ShapeDtypeStruct((bsz, n_rows, cols), BF16),
        compiler_params=_cparams("arbitrary", "arbitrary"),
        name=name,
    )(*operands)


def _qkv(h, w_in_o, n_ctx, qk_dim):
    t_all = h.shape[1]
    n_lat = t_all - n_ctx
    w_bf = w_in_o.astype(BF16)
    cos, su, sd = _rope_tables(n_ctx, n_lat, qk_dim)
    q_scale = qk_dim ** -0.5 * math.log2(math.e)
    q = _projection(h, w_bf, 0, n_ctx, n_lat, (cos[n_ctx:], su[n_ctx:], sd[n_ctx:]), q_scale, qk_dim, "q_rope")
    k = _projection(h, w_bf, 1, 0, t_all, (cos, su, sd), 1.0, qk_dim, "k_rope")
    v = _projection(h, w_bf, 2, 0, t_all, None, 1.0, qk_dim, "v_proj")
    return q, k, v


def _attn_kernel(q_ref, k_ref, v_ref, lp_ref, g_ref, o_ref, s_a, e_a, l_a, s_b, e_b, l_b,
                 *, qk_dim, lam_init, tq, kc):
    lp = lp_ref[...]
    lam = (jnp.exp(jnp.sum(lp[0:1] * lp[1:2], axis=-1, keepdims=True))
           - jnp.exp(jnp.sum(lp[2:3] * lp[3:4], axis=-1, keepdims=True)) + lam_init)
    t_all = k_ref.shape[1]
    n_kc = t_all // kc
    n_q = q_ref.shape[1] // tq
    half = kc // 2
    nt = (((1,), (1,)), ((), ()))

    def row0(i):
        return i * tq if isinstance(i, int) else pl.multiple_of(i * tq, tq)

    def stage(i_soft, soft, i_val, val):
        pv = [None, None]
        for m in range(2):
            dims = slice(m * qk_dim, (m + 1) * qk_dim)
            if soft is not None:
                s_s, e_s, l_s = soft
                qm = q_ref[0, pl.ds(row0(i_soft), tq), dims]
            mx = None
            for c in range(n_kc):
                keys = slice(c * kc, (c + 1) * kc)
                if val is not None:
                    part = _dot(val[1][m, :, keys], v_ref[0, keys, :])
                    pv[m] = part if pv[m] is None else pv[m] + part
                if soft is not None:
                    s = lax.dot_general(qm, k_ref[0, keys, dims], nt,
                                        preferred_element_type=F32)
                    s_s[m, :, keys] = s
                    cm = jnp.maximum(s[:, :half], s[:, half:])
                    mx = cm if mx is None else jnp.maximum(mx, cm)
            if soft is not None:
                row_max = jnp.max(mx, axis=-1, keepdims=True)
                acc = None
                for c in range(n_kc):
                    keys = slice(c * kc, (c + 1) * kc)
                    e = jnp.exp2(s_s[m, :, keys] - row_max)
                    e_s[m, :, keys] = e.astype(e_s.dtype)
                    part = e[:, :half] + e[:, half:]
                    acc = part if acc is None else acc + part
                l_s[m] = 1.0 / jnp.sum(acc, axis=-1, keepdims=True)
        if val is not None:
            l_v = val[2]
            o = pv[0] * l_v[0] - lam * (pv[1] * l_v[1])
            o = o * lax.rsqrt(jnp.mean(o * o, axis=-1, keepdims=True) + EPS)
            o_ref[0, pl.ds(row0(i_val), tq), :] = (o * g_ref[...] * (1.0 - lam_init)).astype(o_ref.dtype)

    set_a, set_b = (s_a, e_a, l_a), (s_b, e_b, l_b)
    stage(0, set_a, None, None)

    def pair(j, carry):
        stage(2 * j + 1, set_b, 2 * j, set_a)
        stage(2 * j + 2, set_a, 2 * j + 1, set_b)
        return carry

    lax.fori_loop(0, n_q // 2 - 1, pair, 0)
    stage(n_q - 1, set_b, n_q - 2, set_a)
    stage(None, None, n_q - 1, set_b)


def _attention(q, k, v, lam_params, subln_g, qk_dim, lam_init):
    bsz, n_lat, cols = q.shape
    t_all = k.shape[1]
    v_dim = subln_g.shape[-1]
    heads = cols // v_dim
    tq, kc = ATTN_Q_TILE, ATTN_KEY_CHUNK
    assert n_lat % (2 * tq) == 0 and t_all % kc == 0
    buffer_set = [pltpu.VMEM((2, tq, t_all), F32), pltpu.VMEM((2, tq, t_all), BF16),
                  pltpu.VMEM((2, tq, 1), F32)]
    return pl.pallas_call(
        functools.partial(_attn_kernel, qk_dim=qk_dim, lam_init=lam_init, tq=tq, kc=kc),
        grid=(bsz, heads),
        in_specs=[pl.BlockSpec((1, n_lat, v_dim), lambda b, h: (b, 0, h)),
                  pl.BlockSpec((1, t_all, v_dim), lambda b, h: (b, 0, h)),
                  pl.BlockSpec((1, t_all, v_dim), lambda b, h: (b, 0, h)),
                  pl.BlockSpec((4, qk_dim), lambda b, h: (0, 0)),
                  pl.BlockSpec((1, v_dim), lambda b, h: (0, 0))],
        out_specs=pl.BlockSpec((1, n_lat, v_dim), lambda b, h: (b, 0, h)),
        out_shape=jax.ShapeDtypeStruct((bsz, n_lat, cols), BF16),
        scratch_shapes=buffer_set + buffer_set,
        compiler_params=_cparams("arbitrary", "arbitrary"),
        name="diff_attention",
    )(q, k, v, lam_params, subln_g)


def kernel(x, c, ctx, c_ctx, ada_w, ada_b, norm_mix_g, norm_ffn_g, final_g, w_in_e, conv_a_w, conv_b_w, conv_b_b, lru_wa, lru_ba, lru_wi, lru_bi, lru_lam, w_out_e, w_in_o, lam_q1, lam_k1, lam_q2, lam_k2, subln_g, w_out_o, router_w, router_b, exp_w_gate, exp_w_up, exp_w_down):
    bsz, n_lat, d = x.shape
    n_ctx = ctx.shape[1]
    depth = ada_w.shape[0]
    assert depth == 2 and w_in_e.shape[0] == 1 and w_in_o.shape[0] == 1
    assert n_ctx % TOKEN_TILE == 0 and n_lat % TOKEN_TILE == 0 and n_ctx % SCAN_CHUNK == 0
    assert router_w.shape[1] == N_GROUPS * EXPERTS_PER_GROUP
    qk_dim = lam_q1.shape[-1]
    assert qk_dim == LANES

    cond = jnp.concatenate([c, c_ctx[None]], axis=0)
    mod = _ada(cond, ada_w, ada_b).reshape(depth, bsz + 1, 6, d)

    h0 = _prenorm(x, ctx, mod, norm_mix_g, 0)
    za, zb = _mixer0(h0, w_in_e[0], conv_a_w[0], conv_b_w[0], conv_b_b[0], lru_wa[0], lru_ba[0],
                     lru_wi[0], lru_bi[0], lru_lam[0], n_ctx)
    x1, f0, info0, meta0, cnt0 = _outproj0(za, zb, w_out_e[0], x, ctx, mod, norm_ffn_g, router_w, router_b, 0)
    expert_w = (exp_w_gate.astype(BF16), exp_w_up.astype(BF16), exp_w_down.astype(BF16))
    o0, pos0 = _moe_sorted_outputs(f0, meta0, cnt0, expert_w, 0)
    x2, h1 = _combine0(o0, pos0, info0, x1, mod, norm_mix_g, 0, n_ctx)

    lam_init = 0.8 - 0.6 * math.exp(-0.3 * 1)
    q, k, v = _qkv(h1, w_in_o[0], n_ctx, qk_dim)
    lam_params = jnp.concatenate([lam_q1, lam_k1, lam_q2, lam_k2], axis=0)
    att = _attention(q, k, v, lam_params, subln_g, qk_dim, lam_init)
    x3, f1, info1, meta1, cnt1 = _outproj1(att, w_out_o[0], x2, mod, norm_ffn_g, router_w, router_b, 1, n_ctx)
    o1, pos1 = _moe_sorted_outputs(f1, meta1, cnt1, expert_w, 1)
    return _combine1(o1, pos1, info1, x3, mod, final_g, 1)
```

```python
import functools
import math

import numpy as np
import jax
import jax.numpy as jnp
from jax import lax
from jax.experimental import pallas as pl
from jax.experimental.pallas import tpu as pltpu

F32 = jnp.float32
BF16 = jnp.bfloat16

EPS = 1e-6
GRID_W = 64
LRU_C = 8.0
ROPE_BASE = 10000.0
N_GROUPS = 4
EXPERTS_PER_GROUP = 4
PAIRS_PER_GROUP = 6
N_CLASSES = N_GROUPS * PAIRS_PER_GROUP
LANES = 128
SUBLANES = 8
TOKEN_TILE = 256
EXPERT_TILE = 256
SCAN_CHUNK = 128
ATTN_Q_TILE = 256
ATTN_KEY_CHUNK = 256
PROJ_TILES = (768, 512, 256)
PROJ_COL_GROUP = 512
DMA_ISSUE_UNROLL = 8
VMEM_LIMIT_BYTES = 56 * 1024 * 1024

_PAIRS = [(0, 1), (0, 2), (0, 3), (1, 2), (1, 3), (2, 3)]
_CLASS_LO = np.array([4 * g + i for g in range(N_GROUPS) for (i, j) in _PAIRS], np.int32)
_CLASS_HI = np.array([4 * g + j for g in range(N_GROUPS) for (i, j) in _PAIRS], np.int32)


def _cparams(*sem):
    return pltpu.CompilerParams(dimension_semantics=sem, vmem_limit_bytes=VMEM_LIMIT_BYTES)


def _pick_tile(n, prefs):
    for t in prefs:
        if n % t == 0:
            return t
    raise ValueError(f"no tile in {prefs} divides {n}")


def _sigmoid(v):
    return 1.0 / (1.0 + jnp.exp(-v))


def _dot(a, b):
    return jnp.dot(a, b, preferred_element_type=F32)


def _mod_norm(xv, g, sc, sh):
    ms = jnp.mean(xv * xv, axis=-1, keepdims=True)
    return xv * lax.rsqrt(ms + EPS) * g * (1.0 + sc) + sh


def _ada_kernel(c_ref, w_ref, b_ref, o_ref):
    cv = c_ref[...]
    s = (cv * _sigmoid(cv)).astype(BF16)
    o_ref[0] = _dot(s, w_ref[0].astype(BF16)) + b_ref[0]


def _ada(cond, ada_w, ada_b):
    depth, d, n6 = ada_w.shape
    rows = cond.shape[0]
    tn = min(1024, n6)
    return pl.pallas_call(
        _ada_kernel,
        grid=(depth, n6 // tn),
        in_specs=[pl.BlockSpec((rows, d), lambda l, j: (0, 0)),
                  pl.BlockSpec((1, d, tn), lambda l, j: (l, 0, j)),
                  pl.BlockSpec((1, 1, tn), lambda l, j: (l, 0, j))],
        out_specs=pl.BlockSpec((1, rows, tn), lambda l, j: (l, 0, j)),
        out_shape=jax.ShapeDtypeStruct((depth, rows, n6), F32),
        compiler_params=_cparams("arbitrary", "arbitrary"),
        name="ada_modulation",
    )(cond, ada_w, ada_b.reshape(depth, 1, n6))


def _prenorm_kernel(x_ref, ctx_ref, mod_ref, g_ref, o_ref, *, n_ctx_tiles):
    i = pl.program_id(1)
    mod = mod_ref[0, 0]

    def run(src_ref):
        o_ref[0] = _mod_norm(src_ref[0], g_ref[...], mod[1:2], mod[0:1]).astype(o_ref.dtype)

    @pl.when(i < n_ctx_tiles)
    def _():
        run(ctx_ref)

    @pl.when(i >= n_ctx_tiles)
    def _():
        run(x_ref)


def _dual_specs(bsz, tm, d, nct):
    x_spec = pl.BlockSpec((1, tm, d), lambda b, i: (b, jnp.maximum(i - nct, 0), 0))
    c_spec = pl.BlockSpec((1, tm, d), lambda b, i: (b, jnp.minimum(i, nct - 1), 0))
    return x_spec, c_spec


def _mod_spec(layer, bsz, nct, d):
    return pl.BlockSpec((1, 1, 6, d), lambda b, i: (layer, jnp.where(i < nct, bsz, b), 0, 0))


def _prenorm(x, ctx, mod, g, layer):
    bsz, n_lat, d = x.shape
    n_ctx = ctx.shape[1]
    tm = TOKEN_TILE
    nct = n_ctx // tm
    t_all = n_ctx + n_lat
    x_spec, c_spec = _dual_specs(bsz, tm, d, nct)
    return pl.pallas_call(
        functools.partial(_prenorm_kernel, n_ctx_tiles=nct),
        grid=(bsz, t_all // tm),
        in_specs=[x_spec, c_spec, _mod_spec(layer, bsz, nct, d),
                  pl.BlockSpec((1, d), lambda b, i: (0, 0))],
        out_specs=pl.BlockSpec((1, tm, d), lambda b, i: (b, i, 0)),
        out_shape=jax.ShapeDtypeStruct((bsz, t_all, d), BF16),
        compiler_params=_cparams("arbitrary", "arbitrary"),
        name="prenorm0",
    )(x, ctx, mod, g[layer:layer + 1])


def _tile_scan(a, b, row, reverse):
    for s in (1, 2, 4):
        if reverse:
            a_s = pltpu.roll(a, SUBLANES - s, 0)
            b_s = pltpu.roll(b, SUBLANES - s, 0)
            ok = row < SUBLANES - s
        else:
            a_s = pltpu.roll(a, s, 0)
            b_s = pltpu.roll(b, s, 0)
            ok = row >= s
        a_s = jnp.where(ok, a_s, 1.0)
        b_s = jnp.where(ok, b_s, 0.0)
        b = a * b_s + b
        a = a * a_s
    return a, b


def _chunk_scan(a_ref, b_ref, r0, hcar, row8, reverse):
    n_tiles = SCAN_CHUNK // SUBLANES
    a = a_ref[pl.ds(r0, SCAN_CHUNK), :]
    b = b_ref[pl.ds(r0, SCAN_CHUNK), :]
    edge = 0 if reverse else SUBLANES - 1
    order = reversed(range(n_tiles)) if reverse else range(n_tiles)
    for v in order:
        at, bt = _tile_scan(a[v * SUBLANES:(v + 1) * SUBLANES], b[v * SUBLANES:(v + 1) * SUBLANES], row8, reverse)
        b_ref[pl.ds(r0 + v * SUBLANES, SUBLANES), :] = at * hcar + bt
        hcar = (jnp.broadcast_to(at[edge:edge + 1], (SUBLANES, LANES)) * hcar
                + jnp.broadcast_to(bt[edge:edge + 1], (SUBLANES, LANES)))
    return hcar


def _gelu_tanh(v):
    return 0.5 * v * (1.0 + jnp.tanh(math.sqrt(2.0 / math.pi) * (v + 0.044715 * (v * v * v))))


def _mixer_kernel(h_ref, w_ref, gw_ref, p_ref, za_ref, zb_ref, y_s, g_s, af_s, bf_s, ab_s, bb_s,
                  *, t_all, n_ctx, mm_rows):
    ch = SCAN_CHUNK
    n_sub = t_all // ch
    n_mm = t_all // mm_rows
    halo = SUBLANES
    ext = ch + 2 * halo
    zero_halo = jnp.zeros((halo, y_s.shape[1]), F32)
    y_s[pl.ds(0, halo), :] = zero_halo
    y_s[pl.ds(halo + t_all, halo), :] = zero_halo

    prm = p_ref[0]
    ca = [prm[k:k + 1] for k in range(3)]
    cb = [prm[3 + k:4 + k] for k in range(4)]
    cb_bias = prm[7:8]
    dirs = [(prm[8:9], prm[9:10], -LRU_C * jnp.log1p(jnp.exp(-prm[10:11])), af_s, bf_s),
            (prm[11:12], prm[12:13], -LRU_C * jnp.log1p(jnp.exp(-prm[13:14])), ab_s, bb_s)]
    row8 = lax.broadcasted_iota(jnp.int32, (SUBLANES, LANES), 0)

    def mm(c):
        r0 = pl.multiple_of(c * mm_rows, mm_rows)
        y_s[pl.ds(halo + r0, mm_rows), :] = _dot(h_ref[0, pl.ds(r0, mm_rows), :], w_ref[0])

    def shifted(v_ext, back):
        return pltpu.roll(v_ext, back % ext, 0)[halo:halo + ch]

    def stage_a(e):
        r0 = e * ch if isinstance(e, int) else pl.multiple_of(e * ch, ch)
        rows = r0 + lax.broadcasted_iota(jnp.int32, (ch, LANES), 0)
        first = (rows == 0) | (rows == n_ctx)
        second = (rows == 1) | (rows == n_ctx + 1)
        last = (rows == n_ctx - 1) | (rows == t_all - 1)
        prod = y_s[pl.ds(r0, ext), LANES:2 * LANES] * y_s[pl.ds(r0, ext), 2 * LANES:3 * LANES]
        conv = (ca[0] * jnp.where(first, 0.0, shifted(prod, 1))
                + ca[1] * prod[halo:halo + ch]
                + ca[2] * jnp.where(last, 0.0, shifted(prod, -1)))
        za_ref[0, pl.ds(r0, ch), :] = (y_s[pl.ds(halo + r0, ch), 0:LANES] * conv).astype(za_ref.dtype)
        g_s[pl.ds(r0, ch), :] = _gelu_tanh(y_s[pl.ds(halo + r0, ch), 3 * LANES:4 * LANES])
        xb = y_s[pl.ds(r0, ext), 4 * LANES:5 * LANES]
        u = (cb[0] * jnp.where(first | second, 0.0, shifted(xb, 2))
             + cb[1] * jnp.where(first, 0.0, shifted(xb, 1))
             + cb[2] * xb[halo:halo + ch]
             + cb[3] * jnp.where(last, 0.0, shifted(xb, -1))
             + cb_bias)
        gates = _dot(u.astype(BF16), gw_ref[0])
        for k, (ba, bi, cl, a_s, b_s) in enumerate(dirs):
            r = _sigmoid(gates[:, 2 * k * LANES:(2 * k + 1) * LANES] + ba)
            ig = _sigmoid(gates[:, (2 * k + 1) * LANES:(2 * k + 2) * LANES] + bi)
            a = jnp.exp(cl * r)
            a_s[pl.ds(r0, ch), :] = a
            b_s[pl.ds(r0, ch), :] = jnp.sqrt(1.0 - a * a) * (ig * u)

    per = mm_rows // ch

    def warmup(c, carry):
        mm(c)
        return carry

    lax.fori_loop(0, 2, warmup, 0)
    stage_a(0)

    def pipelined(k, carry):
        stage_a(per * k - 3)
        stage_a(per * k - 2)
        mm(k)
        return carry

    lax.fori_loop(2, n_mm, pipelined, 0)

    def tail(e, carry):
        stage_a(e)
        return carry

    lax.fori_loop(n_sub - 3, n_sub, tail, 0)

    n_ctx_chunk = n_ctx // ch

    def scans(k, carry):
        hf, hb = carry
        hf = _chunk_scan(af_s, bf_s, pl.multiple_of(k * ch, ch), hf, row8, False)
        c = jnp.where(k < n_ctx_chunk, n_ctx_chunk - 1 - k, n_sub - 1 - (k - n_ctx_chunk))
        hb = _chunk_scan(ab_s, bb_s, pl.multiple_of(c * ch, ch), hb, row8, True)
        return hf, hb

    zero_state = jnp.zeros((SUBLANES, LANES), F32)
    lax.fori_loop(0, n_sub, scans, (zero_state, zero_state))

    def combine(e, carry):
        r0 = pl.multiple_of(e * ch, ch)
        zb_ref[0, pl.ds(r0, ch), :] = (g_s[pl.ds(r0, ch), :]
                                       * (bf_s[pl.ds(r0, ch), :] + bb_s[pl.ds(r0, ch), :])).astype(zb_ref.dtype)
        return carry

    lax.fori_loop(0, n_sub, combine, 0)


def _mixer0(h, w_in, conv_a_w, conv_b_w, conv_b_b, lru_wa, lru_ba, lru_wi, lru_bi, lru_lam, n_ctx):
    bsz, t_all, d = h.shape
    width = conv_a_w.shape[-1]
    heads = width // LANES
    assert lru_wa.shape[1] == heads and lru_wa.shape[2] == LANES
    mm_rows = 2 * SCAN_CHUNK
    assert t_all % mm_rows == 0 and t_all // mm_rows >= 2
    w_r = w_in.reshape(d, 5, heads, LANES).transpose(2, 0, 1, 3).reshape(heads, d, 5 * LANES).astype(BF16)
    gw = jnp.concatenate([lru_wa[0], lru_wi[0], lru_wa[1], lru_wi[1]], axis=-1).astype(BF16)

    def per_head(v):
        return v.reshape(-1, heads, LANES).transpose(1, 0, 2)

    prm = jnp.concatenate([
        per_head(conv_a_w), per_head(conv_b_w), per_head(conv_b_b[None]),
        per_head(lru_ba[0:1]), per_head(lru_bi[0:1]), per_head(lru_lam[0:1]),
        per_head(lru_ba[1:2]), per_head(lru_bi[1:2]), per_head(lru_lam[1:2]),
        jnp.zeros((heads, 2, LANES), F32)], axis=1)
    out_sds = jax.ShapeDtypeStruct((bsz, t_all, width), BF16)
    out_spec = pl.BlockSpec((1, t_all, LANES), lambda b, j: (b, 0, j))
    seq_scratch = pltpu.VMEM((t_all, LANES), F32)
    return pl.pallas_call(
        functools.partial(_mixer_kernel, t_all=t_all, n_ctx=n_ctx, mm_rows=mm_rows),
        grid=(bsz, heads),
        in_specs=[pl.BlockSpec((1, t_all, d), lambda b, j: (b, 0, 0)),
                  pl.BlockSpec((1, d, 5 * LANES), lambda b, j: (j, 0, 0)),
                  pl.BlockSpec((1, LANES, 4 * LANES), lambda b, j: (j, 0, 0)),
                  pl.BlockSpec((1, 16, LANES), lambda b, j: (j, 0, 0))],
        out_specs=[out_spec, out_spec],
        out_shape=[out_sds, out_sds],
        scratch_shapes=[pltpu.VMEM((t_all + 2 * SUBLANES, 5 * LANES), F32),
                        seq_scratch, seq_scratch, seq_scratch, seq_scratch, seq_scratch],
        compiler_params=_cparams("arbitrary", "arbitrary"),
        name="mixer0_conv_lru",
    )(h, w_r, gw, prm)


def _split_router_weights(rw_ref, rw2_ref):
    n_exp = N_GROUPS * EXPERTS_PER_GROUP
    rw = rw_ref[...]
    rw_hi = rw.astype(BF16).astype(F32)
    rw2_ref[...] = (rw_hi + pltpu.roll(rw - rw_hi, n_exp, 1)).astype(rw2_ref.dtype)


def _route(f, rw2_ref, rb_ref, carry_ref, info_ref, meta_ref, cnt_ref, tm):
    n_exp = N_GROUPS * EXPERTS_PER_GROUP
    f_hi = f.astype(BF16)
    f_lo = (f - f_hi.astype(F32)).astype(BF16)
    rw2 = rw2_ref[...]
    lt = (_dot(f_hi, rw2) + _dot(f_lo, rw2)).T
    scores = _sigmoid(lt[0:n_exp] + lt[n_exp:2 * n_exp])
    sel = scores + rb_ref[...]
    s_rows = [scores[e:e + 1] for e in range(n_exp)]
    v_rows = [sel[e:e + 1] for e in range(n_exp)]
    best = None
    gidx = None
    for g in range(N_GROUPS):
        a, b, c, d = v_rows[4 * g:4 * g + 4]
        hi1, lo1 = jnp.maximum(a, b), jnp.minimum(a, b)
        hi2, lo2 = jnp.maximum(c, d), jnp.minimum(c, d)
        gs = jnp.maximum(hi1, hi2) + jnp.maximum(jnp.minimum(hi1, hi2), jnp.maximum(lo1, lo2))
        if g == 0:
            best, gidx = gs, jnp.zeros(gs.shape, jnp.int32)
        else:
            better = gs > best
            gidx = jnp.where(better, g, gidx)
            best = jnp.where(better, gs, best)
    found = None
    for e in range(n_exp):
        g = e // EXPERTS_PER_GROUP
        rank = jnp.zeros(best.shape, jnp.int32)
        for e2 in range(4 * g, 4 * g + 4):
            if e2 == e:
                continue
            ahead = v_rows[e2] > v_rows[e]
            if e2 < e:
                ahead = ahead | (v_rows[e2] == v_rows[e])
            rank = rank + ahead.astype(jnp.int32)
        chosen = (gidx == g) & (rank < 2)
        if e == 0:
            found = chosen
            lo_idx = jnp.zeros(best.shape, jnp.int32)
            hi_idx = jnp.zeros(best.shape, jnp.int32)
            lo_s = s_rows[0]
            hi_s = s_rows[0]
        else:
            take_lo = chosen & jnp.logical_not(found)
            lo_idx = jnp.where(take_lo, e, lo_idx)
            lo_s = jnp.where(take_lo, s_rows[e], lo_s)
            hi_idx = jnp.where(chosen, e, hi_idx)
            hi_s = jnp.where(chosen, s_rows[e], hi_s)
            found = found | chosen
    denom = lo_s + hi_s
    gate_lo = lo_s / denom
    gate_hi = hi_s / denom
    li = lo_idx - 4 * gidx
    hj = hi_idx - 4 * gidx
    pair = jnp.where(li == 0, 0, jnp.where(li == 1, 3, 5)) + hj - li - 1
    cls = PAIRS_PER_GROUP * gidx + pair
    n_cls_pad = 32
    onehot = (lax.broadcasted_iota(jnp.int32, (n_cls_pad, tm), 0) == cls).astype(F32)
    tri = (lax.broadcasted_iota(jnp.int32, (tm, tm), 0)
           < lax.broadcasted_iota(jnp.int32, (tm, tm), 1)).astype(BF16)
    excl = _dot(onehot.astype(BF16), tri)
    rank_in_cls = jnp.sum(onehot * (excl + carry_ref[:, 0:1]), axis=0, keepdims=True)
    carry_ref[...] = carry_ref[...] + jnp.sum(onehot, axis=1, keepdims=True)
    cnt_ref[...] = carry_ref[...]
    cls_f = cls.astype(F32)
    rid = lax.broadcasted_iota(jnp.int32, (LANES, tm), 0)
    packed = jnp.where(rid == 1, gate_lo, jnp.where(rid == 2, gate_hi, 0.0))
    info_ref[...] = packed.T
    rid8 = lax.broadcasted_iota(jnp.int32, (SUBLANES, tm), 0)
    meta_ref[0] = jnp.where(rid8 == 0, cls_f, jnp.where(rid8 == 1, rank_in_cls, 0.0))


def _post_mixer(xin_cols, y_cols, mod_ref, g_ref, rw_ref, rb_ref, outs, scratch, tm):
    x1_ref, f_ref, info_ref, meta_ref, cnt_ref = outs
    carry_ref, rw2_ref = scratch
    first = (pl.program_id(0) == 0) & (pl.program_id(1) == 0)

    @pl.when(first)
    def _():
        carry_ref[...] = jnp.zeros(carry_ref.shape, F32)
        _split_router_weights(rw_ref, rw2_ref)

    mod = mod_ref[0, 0]
    d = mod.shape[-1]
    group = min(PROJ_COL_GROUP, d)
    ssq = None
    for g in range(d // group):
        cols = slice(g * group, (g + 1) * group)
        x1g = xin_cols(cols) + mod[2:3, cols] * y_cols(cols)
        x1_ref[0, :, cols] = x1g
        part = jnp.sum(x1g * x1g, axis=-1, keepdims=True)
        ssq = part if ssq is None else ssq + part
    f = (x1_ref[0] * lax.rsqrt(ssq * (1.0 / d) + EPS) * g_ref[...]) * (1.0 + mod[4:5]) + mod[3:4]
    f_ref[0] = f
    _route(f, rw2_ref, rb_ref, carry_ref, info_ref, meta_ref, cnt_ref, tm)


def _outproj0_kernel(za_ref, zb_ref, wa_ref, wb_ref, x_ref, ctx_ref, mod_ref, g_ref, rw_ref, rb_ref,
                     x1_ref, f_ref, info_ref, meta_ref, cnt_ref, carry_ref, rw2_ref, *, n_ctx_tiles, tm):
    is_ctx = pl.program_id(1) < n_ctx_tiles
    za, zb = za_ref[0], zb_ref[0]
    _post_mixer(lambda cols: jnp.where(is_ctx, ctx_ref[0, :, cols], x_ref[0, :, cols]),
                lambda cols: _dot(za, wa_ref[:, cols]) + _dot(zb, wb_ref[:, cols]),
                mod_ref, g_ref, rw_ref, rb_ref, (x1_ref, f_ref, info_ref, meta_ref, cnt_ref),
                (carry_ref, rw2_ref), tm)


def _outproj1_kernel(z_ref, w_ref, x_ref, mod_ref, g_ref, rw_ref, rb_ref,
                     x1_ref, f_ref, info_ref, meta_ref, cnt_ref, carry_ref, rw2_ref, *, tm):
    z = z_ref[0]
    _post_mixer(lambda cols: x_ref[0, :, cols], lambda cols: _dot(z, w_ref[:, cols]),
                mod_ref, g_ref, rw_ref, rb_ref, (x1_ref, f_ref, info_ref, meta_ref, cnt_ref),
                (carry_ref, rw2_ref), tm)


def _router_operands(router_w, router_b):
    d, n_exp = router_w.shape
    rw = jnp.zeros((d, LANES), F32).at[:, :n_exp].set(router_w)
    return rw, router_b.reshape(n_exp, 1)


def _post_outs(bsz, tiles, tm, d):
    n_tok = bsz * tiles * tm
    out_specs = [pl.BlockSpec((1, tm, d), lambda b, i: (b, i, 0)),
                 pl.BlockSpec((1, tm, d), lambda b, i: (b, i, 0)),
                 pl.BlockSpec((tm, LANES), lambda b, i: (b * tiles + i, 0)),
                 pl.BlockSpec((1, SUBLANES, tm), lambda b, i: (b * tiles + i, 0, 0)),
                 pl.BlockSpec((32, LANES), lambda b, i: (0, 0))]
    out_shape = [jax.ShapeDtypeStruct((bsz, tiles * tm, d), F32),
                 jax.ShapeDtypeStruct((bsz, tiles * tm, d), F32),
                 jax.ShapeDtypeStruct((n_tok, LANES), F32),
                 jax.ShapeDtypeStruct((bsz * tiles, SUBLANES, tm), F32),
                 jax.ShapeDtypeStruct((32, LANES), F32)]
    return out_specs, out_shape


def _outproj0(za, zb, w_out, x, ctx, mod, g_ffn, router_w, router_b, layer):
    bsz, t_all, width = za.shape
    d = x.shape[-1]
    tm = TOKEN_TILE
    nct = ctx.shape[1] // tm
    tiles = t_all // tm
    w_bf = w_out.astype(BF16)
    rw, rb = _router_operands(router_w, router_b)
    x_spec, c_spec = _dual_specs(bsz, tm, d, nct)
    z_spec = pl.BlockSpec((1, tm, width), lambda b, i: (b, i, 0))
    out_specs, out_shape = _post_outs(bsz, tiles, tm, d)
    return pl.pallas_call(
        functools.partial(_outproj0_kernel, n_ctx_tiles=nct, tm=tm),
        grid=(bsz, tiles),
        in_specs=[z_spec, z_spec,
                  pl.BlockSpec((width, d), lambda b, i: (0, 0)),
                  pl.BlockSpec((width, d), lambda b, i: (1, 0)),
                  x_spec, c_spec, _mod_spec(layer, bsz, nct, d),
                  pl.BlockSpec((1, d), lambda b, i: (0, 0)),
                  pl.BlockSpec((d, LANES), lambda b, i: (0, 0)),
                  pl.BlockSpec(rb.shape, lambda b, i: (0, 0))],
        out_specs=out_specs,
        out_shape=out_shape,
        scratch_shapes=[pltpu.VMEM((32, LANES), F32), pltpu.VMEM((d, LANES), BF16)],
        compiler_params=_cparams("arbitrary", "arbitrary"),
        name="outproj0_router",
    )(za, zb, w_bf, w_bf, x, ctx, mod, g_ffn[layer:layer + 1], rw, rb)


def _outproj1(z, w_out, x_all, mod, g_ffn, router_w, router_b, layer, n_ctx):
    bsz, n_lat, width = z.shape
    d = x_all.shape[-1]
    tm = TOKEN_TILE
    nct = n_ctx // tm
    tiles = n_lat // tm
    rw, rb = _router_operands(router_w, router_b)
    out_specs, out_shape = _post_outs(bsz, tiles, tm, d)
    return pl.pallas_call(
        functools.partial(_outproj1_kernel, tm=tm),
        grid=(bsz, tiles),
        in_specs=[pl.BlockSpec((1, tm, width), lambda b, i: (b, i, 0)),
                  pl.BlockSpec((width, d), lambda b, i: (0, 0)),
                  pl.BlockSpec((1, tm, d), lambda b, i: (b, i + nct, 0)),
                  pl.BlockSpec((1, 1, 6, d), lambda b, i: (layer, b, 0, 0)),
                  pl.BlockSpec((1, d), lambda b, i: (0, 0)),
                  pl.BlockSpec((d, LANES), lambda b, i: (0, 0)),
                  pl.BlockSpec(rb.shape, lambda b, i: (0, 0))],
        out_specs=out_specs,
        out_shape=out_shape,
        scratch_shapes=[pltpu.VMEM((32, LANES), F32), pltpu.VMEM((d, LANES), BF16)],
        compiler_params=_cparams("arbitrary", "arbitrary"),
        name="outproj1_router",
    )(z, w_out.astype(BF16), x_all, mod, g_ffn[layer:layer + 1], rw, rb)


def _dispatch_plan(meta, counts, n_tok, tme):
    cls = meta[:, 0, :].reshape(-1).astype(jnp.int32)
    rank = meta[:, 1, :].reshape(-1).astype(jnp.int32)
    cnt = counts[:N_CLASSES, 0].astype(jnp.int32)
    padded = ((cnt + tme - 1) // tme) * tme
    ends = jnp.cumsum(padded)
    starts = ends - padded
    pos = starts[cls] + rank
    r_pad = n_tok + N_CLASSES * tme
    n_tiles = r_pad // tme
    cls_tiles = padded // tme
    cls_tile_start = starts // tme
    used_tiles = ends[-1] // tme
    spare = used_tiles + jnp.arange(N_CLASSES, dtype=jnp.int32)
    fill_tile = jnp.concatenate([ends // tme - 1, spare])
    fill_ok = jnp.concatenate([cnt > 0, spare < n_tiles])
    fill = (jnp.where(fill_ok, fill_tile, 0).astype(jnp.int32), fill_ok.astype(jnp.int32))
    seg_cls = np.array([[c for c in range(N_CLASSES) if _CLASS_LO[c] == e or _CLASS_HI[c] == e]
                        for e in range(N_GROUPS * EXPERTS_PER_GROUP)], np.int32).reshape(-1)
    seg_exp = np.repeat(np.arange(N_GROUPS * EXPERTS_PER_GROUP, dtype=np.int32), 3)
    seg_slot = (_CLASS_HI[seg_cls] == seg_exp).astype(np.int32)
    seg_len = cls_tiles[seg_cls]
    seg_end = jnp.cumsum(seg_len)
    seg_start = seg_end - seg_len
    total = seg_end[-1]
    n_steps = 2 * n_tiles
    q = jnp.arange(n_steps, dtype=jnp.int32)
    qc = jnp.minimum(q, total - 1)
    seg = jnp.sum((seg_end[None, :] <= qc[:, None]).astype(jnp.int32), axis=1)
    real = q < total
    extra = q - total
    step_tile = jnp.where(real, cls_tile_start[seg_cls][seg] + (qc - seg_start[seg]), total // 2 + extra // 2)
    step_exp = jnp.asarray(seg_exp)[seg]
    step_slot = jnp.where(real, jnp.asarray(seg_slot)[seg], extra % 2)
    step_valid = real.astype(jnp.int32)
    return pos, fill, r_pad, (step_exp, step_tile, step_slot, step_valid)


def _gather_rows(idx_ref, nxt_ref, src_hbm, buf, sem, rows, step, n_steps):
    def issue(ref, slot):
        def body(g, carry):
            for u in range(DMA_ISSUE_UNROLL):
                r = g * DMA_ISSUE_UNROLL + u
                pltpu.make_async_copy(src_hbm.at[pl.ds(ref[0, 0, r], 1), :], buf.at[slot, pl.ds(r, 1), :],
                                      sem.at[slot]).start(priority=u % 2)
            return carry
        lax.fori_loop(0, rows // DMA_ISSUE_UNROLL, body, 0)

    slot = step % 2

    @pl.when(step == 0)
    def _():
        issue(idx_ref, 0)

    @pl.when(step + 1 < n_steps)
    def _():
        issue(nxt_ref, 1 - slot)

    pltpu.make_async_copy(src_hbm.at[pl.ds(0, rows), :], buf.at[slot], sem.at[slot]).wait()
    return slot


def _scatter_sorted_kernel(fill_tile_ref, fill_ok_ref, pos_ref, f_ref, xs_hbm, zero_s, sem_z, sem,
                           *, rows, tme, n_fill):
    @pl.when(pl.program_id(0) == 0)
    def _():
        zero_s[...] = jnp.zeros(zero_s.shape, zero_s.dtype)

        def fill(k):
            r0 = pl.multiple_of(fill_tile_ref[k] * tme, tme)
            return pltpu.make_async_copy(zero_s, xs_hbm.at[pl.ds(r0, tme), :], sem_z)

        def start(k, carry):
            @pl.when(fill_ok_ref[k] == 1)
            def _():
                fill(k).start()
            return carry

        def wait(k, carry):
            @pl.when(fill_ok_ref[k] == 1)
            def _():
                fill(k).wait()
            return carry

        lax.fori_loop(0, n_fill, start, 0)
        lax.fori_loop(0, n_fill, wait, 0)

    def body(g, carry):
        for u in range(DMA_ISSUE_UNROLL):
            r = g * DMA_ISSUE_UNROLL + u
            pltpu.make_async_copy(f_ref.at[pl.ds(r, 1), :], xs_hbm.at[pl.ds(pos_ref[0, 0, r], 1), :],
                                  sem).start(priority=u % 2)
        return carry

    lax.fori_loop(0, rows // DMA_ISSUE_UNROLL, body, 0)
    pltpu.make_async_copy(f_ref, xs_hbm.at[pl.ds(0, rows), :], sem).wait()


def _scatter_sorted(f_flat, pos, fill, r_pad, tme):
    n_tok, d = f_flat.shape
    tm = TOKEN_TILE
    n_steps = n_tok // tm
    fill_tile, fill_ok = fill
    grid_spec = pltpu.PrefetchScalarGridSpec(
        num_scalar_prefetch=2,
        grid=(n_steps,),
        in_specs=[pl.BlockSpec((1, 1, tm), lambda i, ft, fo: (i, 0, 0), memory_space=pltpu.SMEM),
                  pl.BlockSpec((tm, d), lambda i, ft, fo: (i, 0))],
        out_specs=pl.BlockSpec(memory_space=pl.ANY),
        scratch_shapes=[pltpu.VMEM((tme, d), F32), pltpu.SemaphoreType.DMA, pltpu.SemaphoreType.DMA],
    )
    return pl.pallas_call(
        functools.partial(_scatter_sorted_kernel, rows=tm, tme=tme, n_fill=fill_tile.shape[0]),
        grid_spec=grid_spec,
        out_shape=jax.ShapeDtypeStruct((r_pad, d), F32),
        compiler_params=_cparams("arbitrary"),
        name="moe_scatter_sorted",
    )(fill_tile, fill_ok, pos.reshape(n_steps, 1, tm), f_flat)


def _expert_kernel(exp_ref, tile_ref, slot_ref, valid_ref, x_ref, wg_ref, wu_ref, wd_ref, o_ref):
    s = pl.program_id(0)

    @pl.when(valid_ref[s] == 1)
    def _():
        xv = x_ref[...].astype(BF16)
        hg = _dot(xv, wg_ref[0, 0])
        hu = _dot(xv, wu_ref[0, 0])
        hid = (hg * _sigmoid(hg) * hu).astype(BF16)
        o_ref[...] = _dot(hid, wd_ref[0, 0])

    @pl.when(valid_ref[s] == 0)
    def _():
        o_ref[...] = jnp.zeros(o_ref.shape, o_ref.dtype)


def _experts(xs, plan, w_gate, w_up, w_down, layer, tme):
    r_pad, d = xs.shape
    ff = w_gate.shape[-1]
    step_exp, step_tile, step_slot, step_valid = plan
    n_steps = step_exp.shape[0]
    grid_spec = pltpu.PrefetchScalarGridSpec(
        num_scalar_prefetch=4,
        grid=(n_steps,),
        in_specs=[pl.BlockSpec((tme, d), lambda s, e, t, w, v: (t[s], 0)),
                  pl.BlockSpec((1, 1, d, ff), lambda s, e, t, w, v: (layer, e[s], 0, 0)),
                  pl.BlockSpec((1, 1, d, ff), lambda s, e, t, w, v: (layer, e[s], 0, 0)),
                  pl.BlockSpec((1, 1, ff, d), lambda s, e, t, w, v: (layer, e[s], 0, 0))],
        out_specs=pl.BlockSpec((tme, d), lambda s, e, t, w, v: (t[s], w[s])),
    )
    return pl.pallas_call(
        _expert_kernel,
        grid_spec=grid_spec,
        out_shape=jax.ShapeDtypeStruct((r_pad, 2 * d), F32),
        compiler_params=_cparams("arbitrary"),
        name="moe_experts",
    )(step_exp, step_tile, step_slot, step_valid, xs, w_gate, w_up, w_down)


def _moe_residual(rows, info_ref, x1_ref, mod_ref, d):
    info = info_ref[...]
    y = info[:, 1:2] * rows[:, 0:d] + info[:, 2:3] * rows[:, d:2 * d]
    return x1_ref[0] + mod_ref[0, 0][5:6] * y


def _linear_step(tiles):
    return pl.program_id(0) * tiles + pl.program_id(1)


def _combine0_kernel(idx_ref, nxt_ref, o_hbm, info_ref, x1_ref, mod_ref, modn_ref, g_ref, x2_ref, h_ref,
                     buf, sem, *, rows, d, tiles, n_steps):
    slot = _gather_rows(idx_ref, nxt_ref, o_hbm, buf, sem, rows, _linear_step(tiles), n_steps)
    x2 = _moe_residual(buf[slot], info_ref, x1_ref, mod_ref, d)
    x2_ref[0] = x2
    modn = modn_ref[0, 0]
    h_ref[0] = _mod_norm(x2, g_ref[...], modn[1:2], modn[0:1]).astype(h_ref.dtype)


def _combine1_kernel(idx_ref, nxt_ref, o_hbm, info_ref, x1_ref, mod_ref, g_ref, out_ref, buf, sem,
                     *, rows, d, tiles, n_steps):
    slot = _gather_rows(idx_ref, nxt_ref, o_hbm, buf, sem, rows, _linear_step(tiles), n_steps)
    x2 = _moe_residual(buf[slot], info_ref, x1_ref, mod_ref, d)
    ms = jnp.mean(x2 * x2, axis=-1, keepdims=True)
    out_ref[0] = x2 * lax.rsqrt(ms + EPS) * g_ref[...]


def _pos_specs(tiles, n_steps, tm):
    cur = pl.BlockSpec((1, 1, tm), lambda b, i: (b * tiles + i, 0, 0), memory_space=pltpu.SMEM)
    nxt = pl.BlockSpec((1, 1, tm), lambda b, i: (jnp.minimum(b * tiles + i + 1, n_steps - 1), 0, 0),
                       memory_space=pltpu.SMEM)
    return cur, nxt


def _combine0(o_sorted, pos, info, x1, mod, norm_mix_g, layer, n_ctx):
    bsz, t_all, d = x1.shape
    tm = TOKEN_TILE
    tiles = t_all // tm
    nct = n_ctx // tm
    n_steps = bsz * tiles
    tok_spec = pl.BlockSpec((1, tm, d), lambda b, i: (b, i, 0))
    cur, nxt = _pos_specs(tiles, n_steps, tm)
    idx = pos.reshape(n_steps, 1, tm)
    return pl.pallas_call(
        functools.partial(_combine0_kernel, rows=tm, d=d, tiles=tiles, n_steps=n_steps),
        grid=(bsz, tiles),
        in_specs=[cur, nxt,
                  pl.BlockSpec(memory_space=pl.ANY),
                  pl.BlockSpec((tm, LANES), lambda b, i: (b * tiles + i, 0)),
                  tok_spec,
                  _mod_spec(layer, bsz, nct, d),
                  _mod_spec(layer + 1, bsz, nct, d),
                  pl.BlockSpec((1, d), lambda b, i: (0, 0))],
        out_specs=[tok_spec, tok_spec],
        out_shape=[jax.ShapeDtypeStruct((bsz, t_all, d), F32), jax.ShapeDtypeStruct((bsz, t_all, d), BF16)],
        scratch_shapes=[pltpu.VMEM((2, tm, 2 * d), F32), pltpu.SemaphoreType.DMA((2,))],
        compiler_params=_cparams("arbitrary", "arbitrary"),
        name="moe_combine0",
    )(idx, idx, o_sorted, info, x1, mod, mod, norm_mix_g[layer + 1:layer + 2])


def _combine1(o_sorted, pos, info, x1, mod, final_g, layer):
    bsz, n_lat, d = x1.shape
    tm = TOKEN_TILE
    tiles = n_lat // tm
    n_steps = bsz * tiles
    tok_spec = pl.BlockSpec((1, tm, d), lambda b, i: (b, i, 0))
    cur, nxt = _pos_specs(tiles, n_steps, tm)
    idx = pos.reshape(n_steps, 1, tm)
    return pl.pallas_call(
        functools.partial(_combine1_kernel, rows=tm, d=d, tiles=tiles, n_steps=n_steps),
        grid=(bsz, tiles),
        in_specs=[cur, nxt,
                  pl.BlockSpec(memory_space=pl.ANY),
                  pl.BlockSpec((tm, LANES), lambda b, i: (b * tiles + i, 0)),
                  tok_spec,
                  pl.BlockSpec((1, 1, 6, d), lambda b, i: (layer, b, 0, 0)),
                  pl.BlockSpec((1, d), lambda b, i: (0, 0))],
        out_specs=tok_spec,
        out_shape=jax.ShapeDtypeStruct((bsz, n_lat, d), F32),
        scratch_shapes=[pltpu.VMEM((2, tm, 2 * d), F32), pltpu.SemaphoreType.DMA((2,))],
        compiler_params=_cparams("arbitrary", "arbitrary"),
        name="moe_combine1",
    )(idx, idx, o_sorted, info, x1, mod, final_g.reshape(1, d))


def _moe_sorted_outputs(f, meta, counts, expert_w, layer):
    d = f.shape[-1]
    f_flat = f.reshape(-1, d)
    n_tok = f_flat.shape[0]
    tme = EXPERT_TILE
    pos, fill, r_pad, plan = _dispatch_plan(meta, counts, n_tok, tme)
    xs = _scatter_sorted(f_flat, pos, fill, r_pad, tme)
    return _experts(xs, plan, *expert_w, layer, tme), pos


def _rope_tables(n_ctx, n_lat, qk_dim):
    axis_dim = qk_dim // 2
    half = axis_dim // 2
    rows = n_lat // GRID_W
    row = jnp.repeat(jnp.arange(rows, dtype=F32), GRID_W)
    col = jnp.tile(jnp.arange(GRID_W, dtype=F32), rows)
    inv = ROPE_BASE ** (-jnp.arange(0, axis_dim, 2, dtype=F32) / axis_dim)
    ang_r = row[:, None] * inv
    ang_c = col[:, None] * inv
    ang = jnp.concatenate([ang_r, ang_r, ang_c, ang_c], axis=-1)
    cos, sin = jnp.cos(ang), jnp.sin(ang)
    lower = (jnp.arange(qk_dim) % axis_dim) < half
    sin_up = jnp.where(lower, -sin, 0.0)
    sin_dn = jnp.where(lower, 0.0, sin)
    pad = jnp.zeros((n_ctx, qk_dim), F32)
    return (jnp.concatenate([pad + 1.0, cos], axis=0), jnp.concatenate([pad, sin_up], axis=0),
            jnp.concatenate([pad, sin_dn], axis=0))


def _proj_kernel(*refs, qk_dim, mult, rope, group):
    if rope:
        h_ref, w_ref, cos_ref, su_ref, sd_ref, o_ref = refs
        cos, su, sd = cos_ref[...], su_ref[...], sd_ref[...]
        half = qk_dim // 4
    else:
        h_ref, w_ref, o_ref = refs
    h = h_ref[0]
    for g in range(w_ref.shape[1] // group):
        y = _dot(h, w_ref[:, g * group:(g + 1) * group])
        if not rope:
            o_ref[0, :, g * group:(g + 1) * group] = y.astype(o_ref.dtype)
            continue
        for k in range(group // qk_dim):
            t = y[:, k * qk_dim:(k + 1) * qk_dim]
            up = pltpu.roll(t, qk_dim - half, 1)
            dn = pltpu.roll(t, half, 1)
            r = t * cos + up * su + dn * sd
            if mult != 1.0:
                r = r * mult
            c0 = g * group + k * qk_dim
            o_ref[0, :, c0:c0 + qk_dim] = r.astype(o_ref.dtype)


def _projection(h, w_bf, col_block, row_off, n_rows, tables, mult, qk_dim, name):
    bsz, _, d = h.shape
    cols = w_bf.shape[1] // 3
    tm = _pick_tile(math.gcd(n_rows, row_off) if row_off else n_rows, PROJ_TILES)
    off = row_off // tm
    tab = pl.BlockSpec((tm, qk_dim), lambda b, i: (i, 0))
    in_specs = [pl.BlockSpec((1, tm, d), lambda b, i: (b, i + off, 0)),
                pl.BlockSpec((d, cols), lambda b, i: (0, col_block))]
    operands = [h, w_bf]
    if tables is not None:
        in_specs += [tab, tab, tab]
        operands += list(tables)
    return pl.pallas_call(
        functools.partial(_proj_kernel, qk_dim=qk_dim, mult=mult, rope=tables is not None,
                          group=min(PROJ_COL_GROUP, cols)),
        grid=(bsz, n_rows // tm),
        in_specs=in_specs,
        out_specs=pl.BlockSpec((1, tm, cols), lambda b, i: (b, i, 0)),
        out_shape=jax.ShapeDtypeStruct((bsz, n_rows, cols), BF16),
        compiler_params=_cparams("arbitrary", "arbitrary"),
        name=name,
    )(*operands)


def _qkv(h, w_in_o, n_ctx, qk_dim):
    t_all = h.shape[1]
    n_lat = t_all - n_ctx
    w_bf = w_in_o.astype(BF16)
    cos, su, sd = _rope_tables(n_ctx, n_lat, qk_dim)
    q_scale = qk_dim ** -0.5 * math.log2(math.e)
    q = _projection(h, w_bf, 0, n_ctx, n_lat, (cos[n_ctx:], su[n_ctx:], sd[n_ctx:]), q_scale, qk_dim, "q_rope")
    k = _projection(h, w_bf, 1, 0, t_all, (cos, su, sd), 1.0, qk_dim, "k_rope")
    v = _projection(h, w_bf, 2, 0, t_all, None, 1.0, qk_dim, "v_proj")
    return q, k, v


def _attn_kernel(q_ref, k_ref, v_ref, lp_ref, g_ref, o_ref, s_a, e_a, l_a, s_b, e_b, l_b,
                 *, qk_dim, lam_init, tq, kc):
    lp = lp_ref[...]
    lam = (jnp.exp(jnp.sum(lp[0:1] * lp[1:2], axis=-1, keepdims=True))
           - jnp.exp(jnp.sum(lp[2:3] * lp[3:4], axis=-1, keepdims=True)) + lam_init)
    t_all = k_ref.shape[1]
    n_kc = t_all // kc
    n_q = q_ref.shape[1] // tq
    half = kc // 2
    nt = (((1,), (1,)), ((), ()))

    def row0(i):
        return i * tq if isinstance(i, int) else pl.multiple_of(i * tq, tq)

    def stage(i_soft, soft, i_val, val):
        pv = [None, None]
        for m in range(2):
            dims = slice(m * qk_dim, (m + 1) * qk_dim)
            if soft is not None:
                s_s, e_s, l_s = soft
                qm = q_ref[0, pl.ds(row0(i_soft), tq), dims]
            mx = None
            for c in range(n_kc):
                keys = slice(c * kc, (c + 1) * kc)
                if val is not None:
                    part = _dot(val[1][m, :, keys], v_ref[0, keys, :])
                    pv[m] = part if pv[m] is None else pv[m] + part
                if soft is not None:
                    s = lax.dot_general(qm, k_ref[0, keys, dims], nt,
                                        preferred_element_type=F32)
                    s_s[m, :, keys] = s
                    cm = jnp.maximum(s[:, :half], s[:, half:])
                    mx = cm if mx is None else jnp.maximum(mx, cm)
            if soft is not None:
                row_max = jnp.max(mx, axis=-1, keepdims=True)
                acc = None
                for c in range(n_kc):
                    keys = slice(c * kc, (c + 1) * kc)
                    e = jnp.exp2(s_s[m, :, keys] - row_max)
                    e_s[m, :, keys] = e.astype(e_s.dtype)
                    part = e[:, :half] + e[:, half:]
                    acc = part if acc is None else acc + part
                l_s[m] = 1.0 / jnp.sum(acc, axis=-1, keepdims=True)
        if val is not None:
            l_v = val[2]
            o = pv[0] * l_v[0] - lam * (pv[1] * l_v[1])
            o = o * lax.rsqrt(jnp.mean(o * o, axis=-1, keepdims=True) + EPS)
            o_ref[0, pl.ds(row0(i_val), tq), :] = (o * g_ref[...] * (1.0 - lam_init)).astype(o_ref.dtype)

    set_a, set_b = (s_a, e_a, l_a), (s_b, e_b, l_b)
    stage(0, set_a, None, None)

    def pair(j, carry):
        stage(2 * j + 1, set_b, 2 * j, set_a)
        stage(2 * j + 2, set_a, 2 * j + 1, set_b)
        return carry

    lax.fori_loop(0, n_q // 2 - 1, pair, 0)
    stage(n_q - 1, set_b, n_q - 2, set_a)
    stage(None, None, n_q - 1, set_b)


def _attention(q, k, v, lam_params, subln_g, qk_dim, lam_init):
    bsz, n_lat, cols = q.shape
    t_all = k.shape[1]
    v_dim = subln_g.shape[-1]
    heads = cols // v_dim
    tq, kc = ATTN_Q_TILE, ATTN_KEY_CHUNK
    assert n_lat % (2 * tq) == 0 and t_all % kc == 0
    buffer_set = [pltpu.VMEM((2, tq, t_all), F32), pltpu.VMEM((2, tq, t_all), BF16),
                  pltpu.VMEM((2, tq, 1), F32)]
    return pl.pallas_call(
        functools.partial(_attn_kernel, qk_dim=qk_dim, lam_init=lam_init, tq=tq, kc=kc),
        grid=(bsz, heads),
        in_specs=[pl.BlockSpec((1, n_lat, v_dim), lambda b, h: (b, 0, h)),
                  pl.BlockSpec((1, t_all, v_dim), lambda b, h: (b, 0, h)),
                  pl.BlockSpec((1, t_all, v_dim), lambda b, h: (b, 0, h)),
                  pl.BlockSpec((4, qk_dim), lambda b, h: (0, 0)),
                  pl.BlockSpec((1, v_dim), lambda b, h: (0, 0))],
        out_specs=pl.BlockSpec((1, n_lat, v_dim), lambda b, h: (b, 0, h)),
        out_shape=jax.ShapeDtypeStruct((bsz, n_lat, cols), BF16),
        scratch_shapes=buffer_set + buffer_set,
        compiler_params=_cparams("arbitrary", "arbitrary"),
        name="diff_attention",
    )(q, k, v, lam_params, subln_g)


def kernel(x, c, ctx, c_ctx, ada_w, ada_b, norm_mix_g, norm_ffn_g, final_g, w_in_e, conv_a_w, conv_b_w, conv_b_b, lru_wa, lru_ba, lru_wi, lru_bi, lru_lam, w_out_e, w_in_o, lam_q1, lam_k1, lam_q2, lam_k2, subln_g, w_out_o, router_w, router_b, exp_w_gate, exp_w_up, exp_w_down):
    bsz, n_lat, d = x.shape
    n_ctx = ctx.shape[1]
    depth = ada_w.shape[0]
    assert depth == 2 and w_in_e.shape[0] == 1 and w_in_o.shape[0] == 1
    assert n_ctx % TOKEN_TILE == 0 and n_lat % TOKEN_TILE == 0 and n_ctx % SCAN_CHUNK == 0
    assert router_w.shape[1] == N_GROUPS * EXPERTS_PER_GROUP
    qk_dim = lam_q1.shape[-1]
    assert qk_dim == LANES

    cond = jnp.concatenate([c, c_ctx[None]], axis=0)
    mod = _ada(cond, ada_w, ada_b).reshape(depth, bsz + 1, 6, d)

    h0 = _prenorm(x, ctx, mod, norm_mix_g, 0)
    za, zb = _mixer0(h0, w_in_e[0], conv_a_w[0], conv_b_w[0], conv_b_b[0], lru_wa[0], lru_ba[0],
                     lru_wi[0], lru_bi[0], lru_lam[0], n_ctx)
    x1, f0, info0, meta0, cnt0 = _outproj0(za, zb, w_out_e[0], x, ctx, mod, norm_ffn_g, router_w, router_b, 0)
    expert_w = (exp_w_gate.astype(BF16), exp_w_up.astype(BF16), exp_w_down.astype(BF16))
    o0, pos0 = _moe_sorted_outputs(f0, meta0, cnt0, expert_w, 0)
    x2, h1 = _combine0(o0, pos0, info0, x1, mod, norm_mix_g, 0, n_ctx)

    lam_init = 0.8 - 0.6 * math.exp(-0.3 * 1)
    q, k, v = _qkv(h1, w_in_o[0], n_ctx, qk_dim)
    lam_params = jnp.concatenate([lam_q1, lam_k1, lam_q2, lam_k2], axis=0)
    att = _attention(q, k, v, lam_params, subln_g, qk_dim, lam_init)
    x3, f1, info1, meta1, cnt1 = _outproj1(att, w_out_o[0], x2, mod, norm_ffn_g, router_w, router_b, 1, n_ctx)
    o1, pos1 = _moe_sorted_outputs(f1, meta1, cnt1, expert_w, 1)
    return _combine1(o1, pos1, info1, x3, mod, final_g, 1)
```

```python
import functools
import math

import numpy as np
import jax
import jax.numpy as jnp
from jax import lax
from jax.experimental import pallas as pl
from jax.experimental.pallas import tpu as pltpu

F32 = jnp.float32
BF16 = jnp.bfloat16

EPS = 1e-6
GRID_W = 64
LRU_C = 8.0
ROPE_BASE = 10000.0
N_GROUPS = 4
EXPERTS_PER_GROUP = 4
PAIRS_PER_GROUP = 6
N_CLASSES = N_GROUPS * PAIRS_PER_GROUP
LANES = 128
SUBLANES = 8
TOKEN_TILE = 256
EXPERT_TILE = 256
SCAN_CHUNK = 128
ATTN_Q_TILE = 256
ATTN_KEY_CHUNK = 256
PROJ_TILES = (768, 512, 256)
PROJ_COL_GROUP = 512
DMA_ISSUE_UNROLL = 8
VMEM_LIMIT_BYTES = 56 * 1024 * 1024

_PAIRS = [(0, 1), (0, 2), (0, 3), (1, 2), (1, 3), (2, 3)]
_CLASS_LO = np.array([4 * g + i for g in range(N_GROUPS) for (i, j) in _PAIRS], np.int32)
_CLASS_HI = np.array([4 * g + j for g in range(N_GROUPS) for (i, j) in _PAIRS], np.int32)


def _cparams(*sem):
    return pltpu.CompilerParams(dimension_semantics=sem, vmem_limit_bytes=VMEM_LIMIT_BYTES)


def _pick_tile(n, prefs):
    for t in prefs:
        if n % t == 0:
            return t
    raise ValueError(f"no tile in {prefs} divides {n}")


def _sigmoid(v):
    return 1.0 / (1.0 + jnp.exp(-v))


def _dot(a, b):
    return jnp.dot(a, b, preferred_element_type=F32)


def _mod_norm(xv, g, sc, sh):
    ms = jnp.mean(xv * xv, axis=-1, keepdims=True)
    return xv * lax.rsqrt(ms + EPS) * g * (1.0 + sc) + sh


def _ada_kernel(c_ref, w_ref, b_ref, o_ref):
    cv = c_ref[...]
    s = (cv * _sigmoid(cv)).astype(BF16)
    o_ref[0] = _dot(s, w_ref[0].astype(BF16)) + b_ref[0]


def _ada(cond, ada_w, ada_b):
    depth, d, n6 = ada_w.shape
    rows = cond.shape[0]
    tn = min(1024, n6)
    return pl.pallas_call(
        _ada_kernel,
        grid=(depth, n6 // tn),
        in_specs=[pl.BlockSpec((rows, d), lambda l, j: (0, 0)),
                  pl.BlockSpec((1, d, tn), lambda l, j: (l, 0, j)),
                  pl.BlockSpec((1, 1, tn), lambda l, j: (l, 0, j))],
        out_specs=pl.BlockSpec((1, rows, tn), lambda l, j: (l, 0, j)),
        out_shape=jax.ShapeDtypeStruct((depth, rows, n6), F32),
        compiler_params=_cparams("arbitrary", "arbitrary"),
        name="ada_modulation",
    )(cond, ada_w, ada_b.reshape(depth, 1, n6))


def _prenorm_kernel(x_ref, ctx_ref, mod_ref, g_ref, o_ref, *, n_ctx_tiles):
    i = pl.program_id(1)
    mod = mod_ref[0, 0]

    def run(src_ref):
        o_ref[0] = _mod_norm(src_ref[0], g_ref[...], mod[1:2], mod[0:1]).astype(o_ref.dtype)

    @pl.when(i < n_ctx_tiles)
    def _():
        run(ctx_ref)

    @pl.when(i >= n_ctx_tiles)
    def _():
        run(x_ref)


def _dual_specs(bsz, tm, d, nct):
    x_spec = pl.BlockSpec((1, tm, d), lambda b, i: (b, jnp.maximum(i - nct, 0), 0))
    c_spec = pl.BlockSpec((1, tm, d), lambda b, i: (b, jnp.minimum(i, nct - 1), 0))
    return x_spec, c_spec


def _mod_spec(layer, bsz, nct, d):
    return pl.BlockSpec((1, 1, 6, d), lambda b, i: (layer, jnp.where(i < nct, bsz, b), 0, 0))


def _prenorm(x, ctx, mod, g, layer):
    bsz, n_lat, d = x.shape
    n_ctx = ctx.shape[1]
    tm = TOKEN_TILE
    nct = n_ctx // tm
    t_all = n_ctx + n_lat
    x_spec, c_spec = _dual_specs(bsz, tm, d, nct)
    return pl.pallas_call(
        functools.partial(_prenorm_kernel, n_ctx_tiles=nct),
        grid=(bsz, t_all // tm),
        in_specs=[x_spec, c_spec, _mod_spec(layer, bsz, nct, d),
                  pl.BlockSpec((1, d), lambda b, i: (0, 0))],
        out_specs=pl.BlockSpec((1, tm, d), lambda b, i: (b, i, 0)),
        out_shape=jax.ShapeDtypeStruct((bsz, t_all, d), BF16),
        compiler_params=_cparams("arbitrary", "arbitrary"),
        name="prenorm0",
    )(x, ctx, mod, g[layer:layer + 1])


def _tile_scan(a, b, row, reverse):
    for s in (1, 2, 4):
        if reverse:
            a_s = pltpu.roll(a, SUBLANES - s, 0)
            b_s = pltpu.roll(b, SUBLANES - s, 0)
            ok = row < SUBLANES - s
        else:
            a_s = pltpu.roll(a, s, 0)
            b_s = pltpu.roll(b, s, 0)
            ok = row >= s
        a_s = jnp.where(ok, a_s, 1.0)
        b_s = jnp.where(ok, b_s, 0.0)
        b = a * b_s + b
        a = a * a_s
    return a, b


def _chunk_scan(a, b, h_ref, r0, hcar, row8, reverse):
    n_tiles = SCAN_CHUNK // SUBLANES
    edge = 0 if reverse else SUBLANES - 1
    order = reversed(range(n_tiles)) if reverse else range(n_tiles)
    for v in order:
        at, bt = _tile_scan(a[v * SUBLANES:(v + 1) * SUBLANES], b[v * SUBLANES:(v + 1) * SUBLANES], row8, reverse)
        h_ref[pl.ds(r0 + v * SUBLANES, SUBLANES), :] = at * hcar + bt
        hcar = (jnp.broadcast_to(at[edge:edge + 1], (SUBLANES, LANES)) * hcar
                + jnp.broadcast_to(bt[edge:edge + 1], (SUBLANES, LANES)))
    return hcar


def _gelu_tanh(v):
    return 0.5 * v * (1.0 + jnp.tanh(math.sqrt(2.0 / math.pi) * (v + 0.044715 * (v * v * v))))


def _mixer_kernel(h_ref, w_ref, gw_ref, p_ref, za_ref, zb_ref, y_s, g_s, hf_s, ab_s, bb_s,
                  *, t_all, n_ctx, mm_rows):
    ch = SCAN_CHUNK
    n_sub = t_all // ch
    n_mm = t_all // mm_rows
    halo = SUBLANES
    ext = ch + 2 * halo
    zero_halo = jnp.zeros((halo, y_s.shape[1]), F32)
    y_s[pl.ds(0, halo), :] = zero_halo
    y_s[pl.ds(halo + t_all, halo), :] = zero_halo

    prm = p_ref[0]
    ca = [prm[k:k + 1] for k in range(3)]
    cb = [prm[3 + k:4 + k] for k in range(4)]
    cb_bias = prm[7:8]
    dirs = [(prm[8:9], prm[9:10], -LRU_C * jnp.log1p(jnp.exp(-prm[10:11]))),
            (prm[11:12], prm[12:13], -LRU_C * jnp.log1p(jnp.exp(-prm[13:14])))]
    row8 = lax.broadcasted_iota(jnp.int32, (SUBLANES, LANES), 0)

    def mm(c):
        r0 = pl.multiple_of(c * mm_rows, mm_rows)
        y_s[pl.ds(halo + r0, mm_rows), :] = _dot(h_ref[0, pl.ds(r0, mm_rows), :], w_ref[0])

    def shifted(v_ext, back):
        return pltpu.roll(v_ext, back % ext, 0)[halo:halo + ch]

    def stage_a(e, hf):
        r0 = e * ch if isinstance(e, int) else pl.multiple_of(e * ch, ch)
        rows = r0 + lax.broadcasted_iota(jnp.int32, (ch, LANES), 0)
        first = (rows == 0) | (rows == n_ctx)
        second = (rows == 1) | (rows == n_ctx + 1)
        last = (rows == n_ctx - 1) | (rows == t_all - 1)
        prod = y_s[pl.ds(r0, ext), LANES:2 * LANES] * y_s[pl.ds(r0, ext), 2 * LANES:3 * LANES]
        conv = (ca[0] * jnp.where(first, 0.0, shifted(prod, 1))
                + ca[1] * prod[halo:halo + ch]
                + ca[2] * jnp.where(last, 0.0, shifted(prod, -1)))
        za_ref[0, pl.ds(r0, ch), :] = (y_s[pl.ds(halo + r0, ch), 0:LANES] * conv).astype(za_ref.dtype)
        g_s[pl.ds(r0, ch), :] = _gelu_tanh(y_s[pl.ds(halo + r0, ch), 3 * LANES:4 * LANES])
        xb = y_s[pl.ds(r0, ext), 4 * LANES:5 * LANES]
        u = (cb[0] * jnp.where(first | second, 0.0, shifted(xb, 2))
             + cb[1] * jnp.where(first, 0.0, shifted(xb, 1))
             + cb[2] * xb[halo:halo + ch]
             + cb[3] * jnp.where(last, 0.0, shifted(xb, -1))
             + cb_bias)
        gates = _dot(u.astype(BF16), gw_ref[0])
        coeffs = []
        for k, (ba, bi, cl) in enumerate(dirs):
            r = _sigmoid(gates[:, 2 * k * LANES:(2 * k + 1) * LANES] + ba)
            ig = _sigmoid(gates[:, (2 * k + 1) * LANES:(2 * k + 2) * LANES] + bi)
            a = jnp.exp(cl * r)
            coeffs.append((a, jnp.sqrt(1.0 - a * a) * (ig * u)))
        ab_s[pl.ds(r0, ch), :] = coeffs[1][0]
        bb_s[pl.ds(r0, ch), :] = coeffs[1][1]
        return _chunk_scan(coeffs[0][0], coeffs[0][1], hf_s, r0, hf, row8, False)

    per = mm_rows // ch

    def warmup(c, carry):
        mm(c)
        return carry

    lax.fori_loop(0, 2, warmup, 0)
    zero_state = jnp.zeros((SUBLANES, LANES), F32)
    hf = stage_a(0, zero_state)

    def pipelined(k, hf):
        hf = stage_a(per * k - 3, hf)
        hf = stage_a(per * k - 2, hf)
        mm(k)
        return hf

    hf = lax.fori_loop(2, n_mm, pipelined, hf)
    lax.fori_loop(n_sub - 3, n_sub, stage_a, hf)

    n_ctx_chunk = n_ctx // ch

    def backward(k, hb):
        c = jnp.where(k < n_ctx_chunk, n_ctx_chunk - 1 - k, n_sub - 1 - (k - n_ctx_chunk))
        r0 = pl.multiple_of(c * ch, ch)
        return _chunk_scan(ab_s[pl.ds(r0, ch), :], bb_s[pl.ds(r0, ch), :], bb_s, r0, hb, row8, True)

    lax.fori_loop(0, n_sub, backward, zero_state)

    def combine(e, carry):
        r0 = pl.multiple_of(e * ch, ch)
        zb_ref[0, pl.ds(r0, ch), :] = (g_s[pl.ds(r0, ch), :]
                                       * (hf_s[pl.ds(r0, ch), :] + bb_s[pl.ds(r0, ch), :])).astype(zb_ref.dtype)
        return carry

    lax.fori_loop(0, n_sub, combine, 0)


def _mixer0(h, w_in, conv_a_w, conv_b_w, conv_b_b, lru_wa, lru_ba, lru_wi, lru_bi, lru_lam, n_ctx):
    bsz, t_all, d = h.shape
    width = conv_a_w.shape[-1]
    heads = width // LANES
    assert lru_wa.shape[1] == heads and lru_wa.shape[2] == LANES
    mm_rows = 2 * SCAN_CHUNK
    assert t_all % mm_rows == 0 and t_all // mm_rows >= 2
    w_r = w_in.reshape(d, 5, heads, LANES).transpose(2, 0, 1, 3).reshape(heads, d, 5 * LANES).astype(BF16)
    gw = jnp.concatenate([lru_wa[0], lru_wi[0], lru_wa[1], lru_wi[1]], axis=-1).astype(BF16)

    def per_head(v):
        return v.reshape(-1, heads, LANES).transpose(1, 0, 2)

    prm = jnp.concatenate([
        per_head(conv_a_w), per_head(conv_b_w), per_head(conv_b_b[None]),
        per_head(lru_ba[0:1]), per_head(lru_bi[0:1]), per_head(lru_lam[0:1]),
        per_head(lru_ba[1:2]), per_head(lru_bi[1:2]), per_head(lru_lam[1:2]),
        jnp.zeros((heads, 2, LANES), F32)], axis=1)
    out_sds = jax.ShapeDtypeStruct((bsz, t_all, width), BF16)
    out_spec = pl.BlockSpec((1, t_all, LANES), lambda b, j: (b, 0, j))
    seq_scratch = pltpu.VMEM((t_all, LANES), F32)
    return pl.pallas_call(
        functools.partial(_mixer_kernel, t_all=t_all, n_ctx=n_ctx, mm_rows=mm_rows),
        grid=(bsz, heads),
        in_specs=[pl.BlockSpec((1, t_all, d), lambda b, j: (b, 0, 0)),
                  pl.BlockSpec((1, d, 5 * LANES), lambda b, j: (j, 0, 0)),
                  pl.BlockSpec((1, LANES, 4 * LANES), lambda b, j: (j, 0, 0)),
                  pl.BlockSpec((1, 16, LANES), lambda b, j: (j, 0, 0))],
        out_specs=[out_spec, out_spec],
        out_shape=[out_sds, out_sds],
        scratch_shapes=[pltpu.VMEM((t_all + 2 * SUBLANES, 5 * LANES), F32),
                        seq_scratch, seq_scratch, seq_scratch, seq_scratch],
        compiler_params=_cparams("arbitrary", "arbitrary"),
        name="mixer0_conv_lru",
    )(h, w_r, gw, prm)


def _split_router_weights(rw_ref, rw2_ref):
    n_exp = N_GROUPS * EXPERTS_PER_GROUP
    rw = rw_ref[...]
    rw_hi = rw.astype(BF16).astype(F32)
    rw2_ref[...] = (rw_hi + pltpu.roll(rw - rw_hi, n_exp, 1)).astype(rw2_ref.dtype)


def _route(f, rw2_ref, rb_ref, carry_ref, info_ref, meta_ref, cnt_ref, tm, count):
    n_exp = N_GROUPS * EXPERTS_PER_GROUP
    f_hi = f.astype(BF16)
    f_lo = (f - f_hi.astype(F32)).astype(BF16)
    rw2 = rw2_ref[...]
    lt = (_dot(f_hi, rw2) + _dot(f_lo, rw2)).T
    scores = _sigmoid(lt[0:n_exp] + lt[n_exp:2 * n_exp])
    sel = scores + rb_ref[...]
    s_rows = [scores[e:e + 1] for e in range(n_exp)]
    v_rows = [sel[e:e + 1] for e in range(n_exp)]
    best = None
    gidx = None
    for g in range(N_GROUPS):
        a, b, c, d = v_rows[4 * g:4 * g + 4]
        hi1, lo1 = jnp.maximum(a, b), jnp.minimum(a, b)
        hi2, lo2 = jnp.maximum(c, d), jnp.minimum(c, d)
        gs = jnp.maximum(hi1, hi2) + jnp.maximum(jnp.minimum(hi1, hi2), jnp.maximum(lo1, lo2))
        if g == 0:
            best, gidx = gs, jnp.zeros(gs.shape, jnp.int32)
        else:
            better = gs > best
            gidx = jnp.where(better, g, gidx)
            best = jnp.where(better, gs, best)
    found = None
    for e in range(n_exp):
        g = e // EXPERTS_PER_GROUP
        rank = jnp.zeros(best.shape, jnp.int32)
        for e2 in range(4 * g, 4 * g + 4):
            if e2 == e:
                continue
            ahead = v_rows[e2] > v_rows[e]
            if e2 < e:
                ahead = ahead | (v_rows[e2] == v_rows[e])
            rank = rank + ahead.astype(jnp.int32)
        chosen = (gidx == g) & (rank < 2)
        if e == 0:
            found = chosen
            lo_idx = jnp.zeros(best.shape, jnp.int32)
            hi_idx = jnp.zeros(best.shape, jnp.int32)
            lo_s = s_rows[0]
            hi_s = s_rows[0]
        else:
            take_lo = chosen & jnp.logical_not(found)
            lo_idx = jnp.where(take_lo, e, lo_idx)
            lo_s = jnp.where(take_lo, s_rows[e], lo_s)
            hi_idx = jnp.where(chosen, e, hi_idx)
            hi_s = jnp.where(chosen, s_rows[e], hi_s)
            found = found | chosen
    denom = lo_s + hi_s
    gate_lo = lo_s / denom
    gate_hi = hi_s / denom
    li = lo_idx - 4 * gidx
    hj = hi_idx - 4 * gidx
    pair = jnp.where(li == 0, 0, jnp.where(li == 1, 3, 5)) + hj - li - 1
    cls = PAIRS_PER_GROUP * gidx + pair
    n_cls_pad = 32
    onehot = (lax.broadcasted_iota(jnp.int32, (n_cls_pad, tm), 0) == cls).astype(F32)
    tri = (lax.broadcasted_iota(jnp.int32, (tm, tm), 0)
           < lax.broadcasted_iota(jnp.int32, (tm, tm), 1)).astype(BF16)
    excl = _dot(onehot.astype(BF16), tri)
    rank_in_cls = jnp.sum(onehot * (excl + carry_ref[:, 0:1]), axis=0, keepdims=True)
    carry_ref[...] = carry_ref[...] + jnp.where(count, jnp.sum(onehot, axis=1, keepdims=True), 0.0)
    cnt_ref[...] = carry_ref[...]
    cls_f = cls.astype(F32)
    rid = lax.broadcasted_iota(jnp.int32, (LANES, tm), 0)
    packed = jnp.where(rid == 1, gate_lo, jnp.where(rid == 2, gate_hi, 0.0))
    info_ref[...] = packed.T
    rid8 = lax.broadcasted_iota(jnp.int32, (SUBLANES, tm), 0)
    meta_ref[0] = jnp.where(rid8 == 0, cls_f, jnp.where(rid8 == 1, rank_in_cls, 0.0))


def _post_mixer(xin_cols, y_cols, mod_ref, g_ref, rw_ref, rb_ref, outs, scratch, tm):
    x1_ref, f_ref, info_ref, meta_ref, cnt_ref = outs
    carry_ref, rw2_ref, y_s = scratch
    s = pl.program_id(0)

    @pl.when(s == 0)
    def _():
        carry_ref[...] = jnp.zeros(carry_ref.shape, F32)
        _split_router_weights(rw_ref, rw2_ref)
        y_s[1] = jnp.zeros(y_s.shape[1:], F32)

    prev, cur = (s + 1) % 2, s % 2
    mod = mod_ref[0, 0]
    d = mod.shape[-1]
    group = min(PROJ_COL_GROUP, d)
    ssq = None
    for g in range(d // group):
        cols = slice(g * group, (g + 1) * group)
        x1g = xin_cols(cols) + mod[2:3, cols] * y_s[prev, :, cols]
        x1_ref[0, :, cols] = x1g
        part = jnp.sum(x1g * x1g, axis=-1, keepdims=True)
        ssq = part if ssq is None else ssq + part
    for g in range(d // group):
        cols = slice(g * group, (g + 1) * group)
        y_s[cur, :, cols] = y_cols(cols)
    f = (x1_ref[0] * lax.rsqrt(ssq * (1.0 / d) + EPS) * g_ref[...]) * (1.0 + mod[4:5]) + mod[3:4]
    f_ref[0] = f
    _route(f, rw2_ref, rb_ref, carry_ref, info_ref, meta_ref, cnt_ref, tm, s > 0)


def _outproj0_kernel(za_ref, zb_ref, wa_ref, wb_ref, x_ref, ctx_ref, mod_ref, g_ref, rw_ref, rb_ref,
                     x1_ref, f_ref, info_ref, meta_ref, cnt_ref, carry_ref, rw2_ref, y_s,
                     *, n_ctx_tiles, tiles, tm):
    is_ctx = (jnp.maximum(pl.program_id(0) - 1, 0) % tiles) < n_ctx_tiles
    za, zb = za_ref[0], zb_ref[0]
    _post_mixer(lambda cols: jnp.where(is_ctx, ctx_ref[0, :, cols], x_ref[0, :, cols]),
                lambda cols: _dot(za, wa_ref[:, cols]) + _dot(zb, wb_ref[:, cols]),
                mod_ref, g_ref, rw_ref, rb_ref, (x1_ref, f_ref, info_ref, meta_ref, cnt_ref),
                (carry_ref, rw2_ref, y_s), tm)


def _outproj1_kernel(z_ref, w_ref, x_ref, mod_ref, g_ref, rw_ref, rb_ref,
                     x1_ref, f_ref, info_ref, meta_ref, cnt_ref, carry_ref, rw2_ref, y_s, *, tm):
    z = z_ref[0]
    _post_mixer(lambda cols: x_ref[0, :, cols], lambda cols: _dot(z, w_ref[:, cols]),
                mod_ref, g_ref, rw_ref, rb_ref, (x1_ref, f_ref, info_ref, meta_ref, cnt_ref),
                (carry_ref, rw2_ref, y_s), tm)


def _router_operands(router_w, router_b):
    d, n_exp = router_w.shape
    rw = jnp.zeros((d, LANES), F32).at[:, :n_exp].set(router_w)
    return rw, router_b.reshape(n_exp, 1)


def _behind(s):
    return jnp.maximum(s - 1, 0)


def _post_outs(bsz, tiles, tm, d):
    n_tok = bsz * tiles * tm
    out_specs = [pl.BlockSpec((1, tm, d), lambda s: (_behind(s) // tiles, _behind(s) % tiles, 0)),
                 pl.BlockSpec((1, tm, d), lambda s: (_behind(s) // tiles, _behind(s) % tiles, 0)),
                 pl.BlockSpec((tm, LANES), lambda s: (_behind(s), 0)),
                 pl.BlockSpec((1, SUBLANES, tm), lambda s: (_behind(s), 0, 0)),
                 pl.BlockSpec((32, LANES), lambda s: (0, 0))]
    out_shape = [jax.ShapeDtypeStruct((bsz, tiles * tm, d), F32),
                 jax.ShapeDtypeStruct((bsz, tiles * tm, d), F32),
                 jax.ShapeDtypeStruct((n_tok, LANES), F32),
                 jax.ShapeDtypeStruct((bsz * tiles, SUBLANES, tm), F32),
                 jax.ShapeDtypeStruct((32, LANES), F32)]
    return out_specs, out_shape


def _outproj0(za, zb, w_out, x, ctx, mod, g_ffn, router_w, router_b, layer):
    bsz, t_all, width = za.shape
    d = x.shape[-1]
    tm = TOKEN_TILE
    nct = ctx.shape[1] // tm
    tiles = t_all // tm
    w_bf = w_out.astype(BF16)
    rw, rb = _router_operands(router_w, router_b)
    n = bsz * tiles

    def ahead(s):
        t = jnp.minimum(s, n - 1)
        return t // tiles, t % tiles

    def behind(s):
        t = _behind(s)
        return t // tiles, t % tiles

    z_spec = pl.BlockSpec((1, tm, width), lambda s: (*ahead(s), 0))
    x_spec = pl.BlockSpec((1, tm, d), lambda s: (behind(s)[0], jnp.maximum(behind(s)[1] - nct, 0), 0))
    c_spec = pl.BlockSpec((1, tm, d), lambda s: (behind(s)[0], jnp.minimum(behind(s)[1], nct - 1), 0))
    mod_spec = pl.BlockSpec((1, 1, 6, d),
                            lambda s: (layer, jnp.where(behind(s)[1] < nct, bsz, behind(s)[0]), 0, 0))
    out_specs, out_shape = _post_outs(bsz, tiles, tm, d)
    return pl.pallas_call(
        functools.partial(_outproj0_kernel, n_ctx_tiles=nct, tiles=tiles, tm=tm),
        grid=(n + 1,),
        in_specs=[z_spec, z_spec,
                  pl.BlockSpec((width, d), lambda s: (0, 0)),
                  pl.BlockSpec((width, d), lambda s: (1, 0)),
                  x_spec, c_spec, mod_spec,
                  pl.BlockSpec((1, d), lambda s: (0, 0)),
                  pl.BlockSpec((d, LANES), lambda s: (0, 0)),
                  pl.BlockSpec(rb.shape, lambda s: (0, 0))],
        out_specs=out_specs,
        out_shape=out_shape,
        scratch_shapes=[pltpu.VMEM((32, LANES), F32), pltpu.VMEM((d, LANES), BF16),
                        pltpu.VMEM((2, tm, d), F32)],
        compiler_params=_cparams("arbitrary"),
        name="outproj0_router",
    )(za, zb, w_bf, w_bf, x, ctx, mod, g_ffn[layer:layer + 1], rw, rb)


def _outproj1(z, w_out, x_all, mod, g_ffn, router_w, router_b, layer, n_ctx):
    bsz, n_lat, width = z.shape
    d = x_all.shape[-1]
    tm = TOKEN_TILE
    nct = n_ctx // tm
    tiles = n_lat // tm
    rw, rb = _router_operands(router_w, router_b)
    n = bsz * tiles
    out_specs, out_shape = _post_outs(bsz, tiles, tm, d)
    return pl.pallas_call(
        functools.partial(_outproj1_kernel, tm=tm),
        grid=(n + 1,),
        in_specs=[pl.BlockSpec((1, tm, width),
                               lambda s: (jnp.minimum(s, n - 1) // tiles, jnp.minimum(s, n - 1) % tiles, 0)),
                  pl.BlockSpec((width, d), lambda s: (0, 0)),
                  pl.BlockSpec((1, tm, d), lambda s: (_behind(s) // tiles, _behind(s) % tiles + nct, 0)),
                  pl.BlockSpec((1, 1, 6, d), lambda s: (layer, _behind(s) // tiles, 0, 0)),
                  pl.BlockSpec((1, d), lambda s: (0, 0)),
                  pl.BlockSpec((d, LANES), lambda s: (0, 0)),
                  pl.BlockSpec(rb.shape, lambda s: (0, 0))],
        out_specs=out_specs,
        out_shape=out_shape,
        scratch_shapes=[pltpu.VMEM((32, LANES), F32), pltpu.VMEM((d, LANES), BF16),
                        pltpu.VMEM((2, tm, d), F32)],
        compiler_params=_cparams("arbitrary"),
        name="outproj1_router",
    )(z, w_out.astype(BF16), x_all, mod, g_ffn[layer:layer + 1], rw, rb)


def _dispatch_plan(meta, counts, n_tok, tme):
    cls = meta[:, 0, :].reshape(-1).astype(jnp.int32)
    rank = meta[:, 1, :].reshape(-1).astype(jnp.int32)
    cnt = counts[:N_CLASSES, 0].astype(jnp.int32)
    padded = ((cnt + tme - 1) // tme) * tme
    ends = jnp.cumsum(padded)
    starts = ends - padded
    pos = starts[cls] + rank
    r_pad = n_tok + N_CLASSES * tme
    n_tiles = r_pad // tme
    cls_tiles = padded // tme
    cls_tile_start = starts // tme
    used_tiles = ends[-1] // tme
    spare = used_tiles + jnp.arange(N_CLASSES, dtype=jnp.int32)
    fill_tile = jnp.concatenate([ends // tme - 1, spare])
    fill_ok = jnp.concatenate([cnt > 0, spare < n_tiles])
    fill = (jnp.where(fill_ok, fill_tile, 0).astype(jnp.int32), fill_ok.astype(jnp.int32))
    seg_cls = np.array([[c for c in range(N_CLASSES) if _CLASS_LO[c] == e or _CLASS_HI[c] == e]
                        for e in range(N_GROUPS * EXPERTS_PER_GROUP)], np.int32).reshape(-1)
    seg_exp = np.repeat(np.arange(N_GROUPS * EXPERTS_PER_GROUP, dtype=np.int32), 3)
    seg_slot = (_CLASS_HI[seg_cls] == seg_exp).astype(np.int32)
    seg_len = cls_tiles[seg_cls]
    seg_end = jnp.cumsum(seg_len)
    seg_start = seg_end - seg_len
    total = seg_end[-1]
    n_steps = 2 * n_tiles
    q = jnp.arange(n_steps, dtype=jnp.int32)
    qc = jnp.minimum(q, total - 1)
    seg = jnp.sum((seg_end[None, :] <= qc[:, None]).astype(jnp.int32), axis=1)
    real = q < total
    extra = q - total
    step_tile = jnp.where(real, cls_tile_start[seg_cls][seg] + (qc - seg_start[seg]), total // 2 + extra // 2)
    step_exp = jnp.asarray(seg_exp)[seg]
    step_slot = jnp.where(real, jnp.asarray(seg_slot)[seg], extra % 2)
    step_valid = real.astype(jnp.int32)
    return pos, fill, r_pad, (step_exp, step_tile, step_slot, step_valid)


def _gather_rows(idx_ref, nxt_ref, src_hbm, buf, sem, rows, step, n_steps):
    def issue(ref, slot):
        def body(g, carry):
            for u in range(DMA_ISSUE_UNROLL):
                r = g * DMA_ISSUE_UNROLL + u
                pltpu.make_async_copy(src_hbm.at[pl.ds(ref[0, 0, r], 1), :], buf.at[slot, pl.ds(r, 1), :],
                                      sem.at[slot]).start(priority=u % 2)
            return carry
        lax.fori_loop(0, rows // DMA_ISSUE_UNROLL, body, 0)

    slot = step % 2

    @pl.when(step == 0)
    def _():
        issue(idx_ref, 0)

    @pl.when(step + 1 < n_steps)
    def _():
        issue(nxt_ref, 1 - slot)

    pltpu.make_async_copy(src_hbm.at[pl.ds(0, rows), :], buf.at[slot], sem.at[slot]).wait()
    return slot


def _scatter_sorted_kernel(fill_tile_ref, fill_ok_ref, pos_ref, f_ref, xs_hbm, zero_s, sem_z, sem,
                           *, rows, tme, n_fill):
    @pl.when(pl.program_id(0) == 0)
    def _():
        zero_s[...] = jnp.zeros(zero_s.shape, zero_s.dtype)

        def fill(k):
            r0 = pl.multiple_of(fill_tile_ref[k] * tme, tme)
            return pltpu.make_async_copy(zero_s, xs_hbm.at[pl.ds(r0, tme), :], sem_z)

        def start(k, carry):
            @pl.when(fill_ok_ref[k] == 1)
            def _():
                fill(k).start()
            return carry

        def wait(k, carry):
            @pl.when(fill_ok_ref[k] == 1)
            def _():
                fill(k).wait()
            return carry

        lax.fori_loop(0, n_fill, start, 0)
        lax.fori_loop(0, n_fill, wait, 0)

    def body(g, carry):
        for u in range(DMA_ISSUE_UNROLL):
            r = g * DMA_ISSUE_UNROLL + u
            pltpu.make_async_copy(f_ref.at[pl.ds(r, 1), :], xs_hbm.at[pl.ds(pos_ref[0, 0, r], 1), :],
                                  sem).start(priority=u % 2)
        return carry

    lax.fori_loop(0, rows // DMA_ISSUE_UNROLL, body, 0)
    pltpu.make_async_copy(f_ref, xs_hbm.at[pl.ds(0, rows), :], sem).wait()


def _scatter_sorted(f_flat, pos, fill, r_pad, tme):
    n_tok, d = f_flat.shape
    tm = TOKEN_TILE
    n_steps = n_tok // tm
    fill_tile, fill_ok = fill
    grid_spec = pltpu.PrefetchScalarGridSpec(
        num_scalar_prefetch=2,
        grid=(n_steps,),
        in_specs=[pl.BlockSpec((1, 1, tm), lambda i, ft, fo: (i, 0, 0), memory_space=pltpu.SMEM),
                  pl.BlockSpec((tm, d), lambda i, ft, fo: (i, 0))],
        out_specs=pl.BlockSpec(memory_space=pl.ANY),
        scratch_shapes=[pltpu.VMEM((tme, d), F32), pltpu.SemaphoreType.DMA, pltpu.SemaphoreType.DMA],
    )
    return pl.pallas_call(
        functools.partial(_scatter_sorted_kernel, rows=tm, tme=tme, n_fill=fill_tile.shape[0]),
        grid_spec=grid_spec,
        out_shape=jax.ShapeDtypeStruct((r_pad, d), F32),
        compiler_params=_cparams("arbitrary"),
        name="moe_scatter_sorted",
    )(fill_tile, fill_ok, pos.reshape(n_steps, 1, tm), f_flat)


def _expert_kernel(exp_ref, tile_ref, slot_ref, valid_ref, x_ref, wg_ref, wu_ref, wd_ref, o_ref):
    s = pl.program_id(0)

    @pl.when(valid_ref[s] == 1)
    def _():
        xv = x_ref[...].astype(BF16)
        hg = _dot(xv, wg_ref[0, 0])
        hu = _dot(xv, wu_ref[0, 0])
        hid = (hg * _sigmoid(hg) * hu).astype(BF16)
        o_ref[...] = _dot(hid, wd_ref[0, 0])

    @pl.when(valid_ref[s] == 0)
    def _():
        o_ref[...] = jnp.zeros(o_ref.shape, o_ref.dtype)


def _experts(xs, plan, w_gate, w_up, w_down, layer, tme):
    r_pad, d = xs.shape
    ff = w_gate.shape[-1]
    step_exp, step_tile, step_slot, step_valid = plan
    n_steps = step_exp.shape[0]
    grid_spec = pltpu.PrefetchScalarGridSpec(
        num_scalar_prefetch=4,
        grid=(n_steps,),
        in_specs=[pl.BlockSpec((tme, d), lambda s, e, t, w, v: (t[s], 0)),
                  pl.BlockSpec((1, 1, d, ff), lambda s, e, t, w, v: (layer, e[s], 0, 0)),
                  pl.BlockSpec((1, 1, d, ff), lambda s, e, t, w, v: (layer, e[s], 0, 0)),
                  pl.BlockSpec((1, 1, ff, d), lambda s, e, t, w, v: (layer, e[s], 0, 0))],
        out_specs=pl.BlockSpec((tme, d), lambda s, e, t, w, v: (t[s], w[s])),
    )
    return pl.pallas_call(
        _expert_kernel,
        grid_spec=grid_spec,
        out_shape=jax.ShapeDtypeStruct((r_pad, 2 * d), F32),
        compiler_params=_cparams("arbitrary"),
        name="moe_experts",
    )(step_exp, step_tile, step_slot, step_valid, xs, w_gate, w_up, w_down)


def _moe_residual(rows, info_ref, x1_ref, mod_ref, d):
    info = info_ref[...]
    y = info[:, 1:2] * rows[:, 0:d] + info[:, 2:3] * rows[:, d:2 * d]
    return x1_ref[0] + mod_ref[0, 0][5:6] * y


def _linear_step(tiles):
    return pl.program_id(0) * tiles + pl.program_id(1)


def _combine0_kernel(idx_ref, nxt_ref, o_hbm, info_ref, x1_ref, mod_ref, modn_ref, g_ref, x2_ref, h_ref,
                     buf, sem, *, rows, d, tiles, n_steps):
    slot = _gather_rows(idx_ref, nxt_ref, o_hbm, buf, sem, rows, _linear_step(tiles), n_steps)
    x2 = _moe_residual(buf[slot], info_ref, x1_ref, mod_ref, d)
    x2_ref[0] = x2
    modn = modn_ref[0, 0]
    h_ref[0] = _mod_norm(x2, g_ref[...], modn[1:2], modn[0:1]).astype(h_ref.dtype)


def _combine1_kernel(idx_ref, nxt_ref, o_hbm, info_ref, x1_ref, mod_ref, g_ref, out_ref, buf, sem,
                     *, rows, d, tiles, n_steps):
    slot = _gather_rows(idx_ref, nxt_ref, o_hbm, buf, sem, rows, _linear_step(tiles), n_steps)
    x2 = _moe_residual(buf[slot], info_ref, x1_ref, mod_ref, d)
    ms = jnp.mean(x2 * x2, axis=-1, keepdims=True)
    out_ref[0] = x2 * lax.rsqrt(ms + EPS) * g_ref[...]


def _pos_specs(tiles, n_steps, tm):
    cur = pl.BlockSpec((1, 1, tm), lambda b, i: (b * tiles + i, 0, 0), memory_space=pltpu.SMEM)
    nxt = pl.BlockSpec((1, 1, tm), lambda b, i: (jnp.minimum(b * tiles + i + 1, n_steps - 1), 0, 0),
                       memory_space=pltpu.SMEM)
    return cur, nxt


def _combine0(o_sorted, pos, info, x1, mod, norm_mix_g, layer, n_ctx):
    bsz, t_all, d = x1.shape
    tm = TOKEN_TILE
    tiles = t_all // tm
    nct = n_ctx // tm
    n_steps = bsz * tiles
    tok_spec = pl.BlockSpec((1, tm, d), lambda b, i: (b, i, 0))
    cur, nxt = _pos_specs(tiles, n_steps, tm)
    idx = pos.reshape(n_steps, 1, tm)
    return pl.pallas_call(
        functools.partial(_combine0_kernel, rows=tm, d=d, tiles=tiles, n_steps=n_steps),
        grid=(bsz, tiles),
        in_specs=[cur, nxt,
                  pl.BlockSpec(memory_space=pl.ANY),
                  pl.BlockSpec((tm, LANES), lambda b, i: (b * tiles + i, 0)),
                  tok_spec,
                  _mod_spec(layer, bsz, nct, d),
                  _mod_spec(layer + 1, bsz, nct, d),
                  pl.BlockSpec((1, d), lambda b, i: (0, 0))],
        out_specs=[tok_spec, tok_spec],
        out_shape=[jax.ShapeDtypeStruct((bsz, t_all, d), F32), jax.ShapeDtypeStruct((bsz, t_all, d), BF16)],
        scratch_shapes=[pltpu.VMEM((2, tm, 2 * d), F32), pltpu.SemaphoreType.DMA((2,))],
        compiler_params=_cparams("arbitrary", "arbitrary"),
        name="moe_combine0",
    )(idx, idx, o_sorted, info, x1, mod, mod, norm_mix_g[layer + 1:layer + 2])


def _combine1(o_sorted, pos, info, x1, mod, final_g, layer):
    bsz, n_lat, d = x1.shape
    tm = TOKEN_TILE
    tiles = n_lat // tm
    n_steps = bsz * tiles
    tok_spec = pl.BlockSpec((1, tm, d), lambda b, i: (b, i, 0))
    cur, nxt = _pos_specs(tiles, n_steps, tm)
    idx = pos.reshape(n_steps, 1, tm)
    return pl.pallas_call(
        functools.partial(_combine1_kernel, rows=tm, d=d, tiles=tiles, n_steps=n_steps),
        grid=(bsz, tiles),
        in_specs=[cur, nxt,
                  pl.BlockSpec(memory_space=pl.ANY),
                  pl.BlockSpec((tm, LANES), lambda b, i: (b * tiles + i, 0)),
                  tok_spec,
                  pl.BlockSpec((1, 1, 6, d), lambda b, i: (layer, b, 0, 0)),
                  pl.BlockSpec((1, d), lambda b, i: (0, 0))],
        out_specs=tok_spec,
        out_shape=jax.ShapeDtypeStruct((bsz, n_lat, d), F32),
        scratch_shapes=[pltpu.VMEM((2, tm, 2 * d), F32), pltpu.SemaphoreType.DMA((2,))],
        compiler_params=_cparams("arbitrary", "arbitrary"),
        name="moe_combine1",
    )(idx, idx, o_sorted, info, x1, mod, final_g.reshape(1, d))


def _moe_sorted_outputs(f, meta, counts, expert_w, layer):
    d = f.shape[-1]
    f_flat = f.reshape(-1, d)
    n_tok = f_flat.shape[0]
    tme = EXPERT_TILE
    pos, fill, r_pad, plan = _dispatch_plan(meta, counts, n_tok, tme)
    xs = _scatter_sorted(f_flat, pos, fill, r_pad, tme)
    return _experts(xs, plan, *expert_w, layer, tme), pos


def _rope_tables(n_ctx, n_lat, qk_dim):
    axis_dim = qk_dim // 2
    half = axis_dim // 2
    rows = n_lat // GRID_W
    row = jnp.repeat(jnp.arange(rows, dtype=F32), GRID_W)
    col = jnp.tile(jnp.arange(GRID_W, dtype=F32), rows)
    inv = ROPE_BASE ** (-jnp.arange(0, axis_dim, 2, dtype=F32) / axis_dim)
    ang_r = row[:, None] * inv
    ang_c = col[:, None] * inv
    ang = jnp.concatenate([ang_r, ang_r, ang_c, ang_c], axis=-1)
    cos, sin = jnp.cos(ang), jnp.sin(ang)
    lower = (jnp.arange(qk_dim) % axis_dim) < half
    sin_up = jnp.where(lower, -sin, 0.0)
    sin_dn = jnp.where(lower, 0.0, sin)
    pad = jnp.zeros((n_ctx, qk_dim), F32)
    return (jnp.concatenate([pad + 1.0, cos], axis=0), jnp.concatenate([pad, sin_up], axis=0),
            jnp.concatenate([pad, sin_dn], axis=0))


def _proj_kernel(*refs, qk_dim, mult, rope, group):
    if rope:
        h_ref, w_ref, cos_ref, su_ref, sd_ref, o_ref = refs
        cos, su, sd = cos_ref[...], su_ref[...], sd_ref[...]
        half = qk_dim // 4
    else:
        h_ref, w_ref, o_ref = refs
    h = h_ref[0]
    for g in range(w_ref.shape[1] // group):
        y = _dot(h, w_ref[:, g * group:(g + 1) * group])
        if not rope:
            o_ref[0, :, g * group:(g + 1) * group] = y.astype(o_ref.dtype)
            continue
        for k in range(group // qk_dim):
            t = y[:, k * qk_dim:(k + 1) * qk_dim]
            up = pltpu.roll(t, qk_dim - half, 1)
            dn = pltpu.roll(t, half, 1)
            r = t * cos + up * su + dn * sd
            if mult != 1.0:
                r = r * mult
            c0 = g * group + k * qk_dim
            o_ref[0, :, c0:c0 + qk_dim] = r.astype(o_ref.dtype)


def _projection(h, w_bf, col_block, row_off, n_rows, tables, mult, qk_dim, name):
    bsz, _, d = h.shape
    cols = w_bf.shape[1] // 3
    tm = _pick_tile(math.gcd(n_rows, row_off) if row_off else n_rows, PROJ_TILES)
    off = row_off // tm
    tab = pl.BlockSpec((tm, qk_dim), lambda b, i: (i, 0))
    in_specs = [pl.BlockSpec((1, tm, d), lambda b, i: (b, i + off, 0)),
                pl.BlockSpec((d, cols), lambda b, i: (0, col_block))]
    operands = [h, w_bf]
    if tables is not None:
        in_specs += [tab, tab, tab]
        operands += list(tables)
    return pl.pallas_call(
        functools.partial(_proj_kernel, qk_dim=qk_dim, mult=mult, rope=tables is not None,
                          group=min(PROJ_COL_GROUP, cols)),
        grid=(bsz, n_rows // tm),
        in_specs=in_specs,
        out_specs=pl.BlockSpec((1, tm, cols), lambda b, i: (b, i, 0)),
        out_shape=jax.ShapeDtypeStruct((bsz, n_rows, cols), BF16),
        compiler_params=_cparams("arbitrary", "arbitrary"),
        name=name,
    )(*operands)


def _qkv(h, w_in_o, n_ctx, qk_dim):
    t_all = h.shape[1]
    n_lat = t_all - n_ctx
    w_bf = w_in_o.astype(BF16)
    cos, su, sd = _rope_tables(n_ctx, n_lat, qk_dim)
    q_scale = qk_dim ** -0.5 * math.log2(math.e)
    q = _projection(h, w_bf, 0, n_ctx, n_lat, (cos[n_ctx:], su[n_ctx:], sd[n_ctx:]), q_scale, qk_dim, "q_rope")
    k = _projection(h, w_bf, 1, 0, t_all, (cos, su, sd), 1.0, qk_dim, "k_rope")
    v = _projection(h, w_bf, 2, 0, t_all, None, 1.0, qk_dim, "v_proj")
    return q, k, v


def _attn_kernel(q_ref, k_ref, v_ref, lp_ref, g_ref, o_ref, s_a, e_a, l_a, s_b, e_b, l_b,
                 *, qk_dim, lam_init, tq, kc):
    lp = lp_ref[...]
    lam = (jnp.exp(jnp.sum(lp[0:1] * lp[1:2], axis=-1, keepdims=True))
           - jnp.exp(jnp.sum(lp[2:3] * lp[3:4], axis=-1, keepdims=True)) + lam_init)
    t_all = k_ref.shape[1]
    n_kc = t_all // kc
    n_q = q_ref.shape[1] // tq
    half = kc // 2
    nt = (((1,), (1,)), ((), ()))

    def row0(i):
        return i * tq if isinstance(i, int) else pl.multiple_of(i * tq, tq)

    def stage(i_soft, soft, i_val, val):
        pv = [None, None]
        for m in range(2):
            dims = slice(m * qk_dim, (m + 1) * qk_dim)
            if soft is not None:
                s_s, e_s, l_s = soft
                qm = q_ref[0, pl.ds(row0(i_soft), tq), dims]
            mx = None
            for c in range(n_kc):
                keys = slice(c * kc, (c + 1) * kc)
                if val is not None:
                    part = _dot(val[1][m, :, keys], v_ref[0, keys, :])
                    pv[m] = part if pv[m] is None else pv[m] + part
                if soft is not None:
                    s = lax.dot_general(qm, k_ref[0, keys, dims], nt,
                                        preferred_element_type=F32)
                    s_s[m, :, keys] = s
                    cm = jnp.maximum(s[:, :half], s[:, half:])
                    mx = cm if mx is None else jnp.maximum(mx, cm)
            if soft is not None:
                row_max = jnp.max(mx, axis=-1, keepdims=True)
                acc = None
                for c in range(n_kc):
                    keys = slice(c * kc, (c + 1) * kc)
                    e = jnp.exp2(s_s[m, :, keys] - row_max)
                    e_s[m, :, keys] = e.astype(e_s.dtype)
                    part = e[:, :half] + e[:, half:]
                    acc = part if acc is None else acc + part
                l_s[m] = 1.0 / jnp.sum(acc, axis=-1, keepdims=True)
        if val is not None:
            l_v = val[2]
            o = pv[0] * l_v[0] - lam * (pv[1] * l_v[1])
            o = o * lax.rsqrt(jnp.mean(o * o, axis=-1, keepdims=True) + EPS)
            o_ref[0, pl.ds(row0(i_val), tq), :] = (o * g_ref[...] * (1.0 - lam_init)).astype(o_ref.dtype)

    set_a, set_b = (s_a, e_a, l_a), (s_b, e_b, l_b)
    stage(0, set_a, None, None)

    def pair(j, carry):
        stage(2 * j + 1, set_b, 2 * j, set_a)
        stage(2 * j + 2, set_a, 2 * j + 1, set_b)
        return carry

    lax.fori_loop(0, n_q // 2 - 1, pair, 0)
    stage(n_q - 1, set_b, n_q - 2, set_a)
    stage(None, None, n_q - 1, set_b)


def _attention(q, k, v, lam_params, subln_g, qk_dim, lam_init):
    bsz, n_lat, cols = q.shape
    t_all = k.shape[1]
    v_dim = subln_g.shape[-1]
    heads = cols // v_dim
    tq, kc = ATTN_Q_TILE, ATTN_KEY_CHUNK
    assert n_lat % (2 * tq) == 0 and t_all % kc == 0
    buffer_set = [pltpu.VMEM((2, tq, t_all), F32), pltpu.VMEM((2, tq, t_all), BF16),
                  pltpu.VMEM((2, tq, 1), F32)]
    return pl.pallas_call(
        functools.partial(_attn_kernel, qk_dim=qk_dim, lam_init=lam_init, tq=tq, kc=kc),
        grid=(bsz, heads),
        in_specs=[pl.BlockSpec((1, n_lat, v_dim), lambda b, h: (b, 0, h)),
                  pl.BlockSpec((1, t_all, v_dim), lambda b, h: (b, 0, h)),
                  pl.BlockSpec((1, t_all, v_dim), lambda b, h: (b, 0, h)),
                  pl.BlockSpec((4, qk_dim), lambda b, h: (0, 0)),
                  pl.BlockSpec((1, v_dim), lambda b, h: (0, 0))],
        out_specs=pl.BlockSpec((1, n_lat, v_dim), lambda b, h: (b, 0, h)),
        out_shape=jax.ShapeDtypeStruct((bsz, n_lat, cols), BF16),
        scratch_shapes=buffer_set + buffer_set,
        compiler_params=_cparams("arbitrary", "arbitrary"),
        name="diff_attention",
    )(q, k, v, lam_params, subln_g)


def kernel(x, c, ctx, c_ctx, ada_w, ada_b, norm_mix_g, norm_ffn_g, final_g, w_in_e, conv_a_w, conv_b_w, conv_b_b, lru_wa, lru_ba, lru_wi, lru_bi, lru_lam, w_out_e, w_in_o, lam_q1, lam_k1, lam_q2, lam_k2, subln_g, w_out_o, router_w, router_b, exp_w_gate, exp_w_up, exp_w_down):
    bsz, n_lat, d = x.shape
    n_ctx = ctx.shape[1]
    depth = ada_w.shape[0]
    assert depth == 2 and w_in_e.shape[0] == 1 and w_in_o.shape[0] == 1
    assert n_ctx % TOKEN_TILE == 0 and n_lat % TOKEN_TILE == 0 and n_ctx % SCAN_CHUNK == 0
    assert router_w.shape[1] == N_GROUPS * EXPERTS_PER_GROUP
    qk_dim = lam_q1.shape[-1]
    assert qk_dim == LANES

    cond = jnp.concatenate([c, c_ctx[None]], axis=0)
    mod = _ada(cond, ada_w, ada_b).reshape(depth, bsz + 1, 6, d)

    h0 = _prenorm(x, ctx, mod, norm_mix_g, 0)
    za, zb = _mixer0(h0, w_in_e[0], conv_a_w[0], conv_b_w[0], conv_b_b[0], lru_wa[0], lru_ba[0],
                     lru_wi[0], lru_bi[0], lru_lam[0], n_ctx)
    x1, f0, info0, meta0, cnt0 = _outproj0(za, zb, w_out_e[0], x, ctx, mod, norm_ffn_g, router_w, router_b, 0)
    expert_w = (exp_w_gate.astype(BF16), exp_w_up.astype(BF16), exp_w_down.astype(BF16))
    o0, pos0 = _moe_sorted_outputs(f0, meta0, cnt0, expert_w, 0)
    x2, h1 = _combine0(o0, pos0, info0, x1, mod, norm_mix_g, 0, n_ctx)

    lam_init = 0.8 - 0.6 * math.exp(-0.3 * 1)
    q, k, v = _qkv(h1, w_in_o[0], n_ctx, qk_dim)
    lam_params = jnp.concatenate([lam_q1, lam_k1, lam_q2, lam_k2], axis=0)
    att = _attention(q, k, v, lam_params, subln_g, qk_dim, lam_init)
    x3, f1, info1, meta1, cnt1 = _outproj1(att, w_out_o[0], x2, mod, norm_ffn_g, router_w, router_b, 1, n_ctx)
    o1, pos1 = _moe_sorted_outputs(f1, meta1, cnt1, expert_w, 1)
    return _combine1(o1, pos1, info1, x3, mod, final_g, 1)
```

```python
import functools
import math

import numpy as np
import jax
import jax.numpy as jnp
from jax import lax
from jax.experimental import pallas as pl
from jax.experimental.pallas import tpu as pltpu

F32 = jnp.float32
BF16 = jnp.bfloat16

EPS = 1e-6
GRID_W = 64
LRU_C = 8.0
ROPE_BASE = 10000.0
N_GROUPS = 4
EXPERTS_PER_GROUP = 4
PAIRS_PER_GROUP = 6
N_CLASSES = N_GROUPS * PAIRS_PER_GROUP
LANES = 128
SUBLANES = 8
TOKEN_TILE = 256
EXPERT_TILE = 256
SCAN_CHUNK = 128
ATTN_Q_TILE = 256
ATTN_KEY_CHUNK = 256
PROJ_TILES = (768, 512, 256)
PROJ_COL_GROUP = 512
EXPERT_CAST_ROWS = 256
DMA_ISSUE_UNROLL = 8
VMEM_LIMIT_BYTES = 56 * 1024 * 1024

_PAIRS = [(0, 1), (0, 2), (0, 3), (1, 2), (1, 3), (2, 3)]
_CLASS_LO = np.array([4 * g + i for g in range(N_GROUPS) for (i, j) in _PAIRS], np.int32)
_CLASS_HI = np.array([4 * g + j for g in range(N_GROUPS) for (i, j) in _PAIRS], np.int32)


def _cparams(*sem):
    return pltpu.CompilerParams(dimension_semantics=sem, vmem_limit_bytes=VMEM_LIMIT_BYTES)


def _pick_tile(n, prefs):
    for t in prefs:
        if n % t == 0:
            return t
    raise ValueError(f"no tile in {prefs} divides {n}")


def _sigmoid(v):
    return 1.0 / (1.0 + jnp.exp(-v))


def _dot(a, b):
    return jnp.dot(a, b, preferred_element_type=F32)


def _mod_norm(xv, g, sc, sh):
    ms = jnp.mean(xv * xv, axis=-1, keepdims=True)
    return xv * lax.rsqrt(ms + EPS) * g * (1.0 + sc) + sh


def _ada_kernel(c_ref, w_ref, b_ref, o_ref):
    cv = c_ref[...]
    s = (cv * _sigmoid(cv)).astype(BF16)
    o_ref[0] = _dot(s, w_ref[0].astype(BF16)) + b_ref[0]


def _ada(cond, ada_w, ada_b):
    depth, d, n6 = ada_w.shape
    rows = cond.shape[0]
    tn = min(1024, n6)
    return pl.pallas_call(
        _ada_kernel,
        grid=(depth, n6 // tn),
        in_specs=[pl.BlockSpec((rows, d), lambda l, j: (0, 0)),
                  pl.BlockSpec((1, d, tn), lambda l, j: (l, 0, j)),
                  pl.BlockSpec((1, 1, tn), lambda l, j: (l, 0, j))],
        out_specs=pl.BlockSpec((1, rows, tn), lambda l, j: (l, 0, j)),
        out_shape=jax.ShapeDtypeStruct((depth, rows, n6), F32),
        compiler_params=_cparams("arbitrary", "arbitrary"),
        name="ada_modulation",
    )(cond, ada_w, ada_b.reshape(depth, 1, n6))


def _prenorm_kernel(x_ref, ctx_ref, mod_ref, g_ref, o_ref, *, n_ctx_tiles):
    i = pl.program_id(1)
    mod = mod_ref[0, 0]

    def run(src_ref):
        o_ref[0] = _mod_norm(src_ref[0], g_ref[...], mod[1:2], mod[0:1]).astype(o_ref.dtype)

    @pl.when(i < n_ctx_tiles)
    def _():
        run(ctx_ref)

    @pl.when(i >= n_ctx_tiles)
    def _():
        run(x_ref)


def _dual_specs(bsz, tm, d, nct):
    x_spec = pl.BlockSpec((1, tm, d), lambda b, i: (b, jnp.maximum(i - nct, 0), 0))
    c_spec = pl.BlockSpec((1, tm, d), lambda b, i: (b, jnp.minimum(i, nct - 1), 0))
    return x_spec, c_spec


def _mod_spec(layer, bsz, nct, d):
    return pl.BlockSpec((1, 1, 6, d), lambda b, i: (layer, jnp.where(i < nct, bsz, b), 0, 0))


def _prenorm(x, ctx, mod, g, layer):
    bsz, n_lat, d = x.shape
    n_ctx = ctx.shape[1]
    tm = TOKEN_TILE
    nct = n_ctx // tm
    t_all = n_ctx + n_lat
    x_spec, c_spec = _dual_specs(bsz, tm, d, nct)
    return pl.pallas_call(
        functools.partial(_prenorm_kernel, n_ctx_tiles=nct),
        grid=(bsz, t_all // tm),
        in_specs=[x_spec, c_spec, _mod_spec(layer, bsz, nct, d),
                  pl.BlockSpec((1, d), lambda b, i: (0, 0))],
        out_specs=pl.BlockSpec((1, tm, d), lambda b, i: (b, i, 0)),
        out_shape=jax.ShapeDtypeStruct((bsz, t_all, d), BF16),
        compiler_params=_cparams("arbitrary", "arbitrary"),
        name="prenorm0",
    )(x, ctx, mod, g[layer:layer + 1])


def _tile_scan(a, b, row, reverse):
    for s in (1, 2, 4):
        if reverse:
            a_s = pltpu.roll(a, SUBLANES - s, 0)
            b_s = pltpu.roll(b, SUBLANES - s, 0)
            ok = row < SUBLANES - s
        else:
            a_s = pltpu.roll(a, s, 0)
            b_s = pltpu.roll(b, s, 0)
            ok = row >= s
        a_s = jnp.where(ok, a_s, 1.0)
        b_s = jnp.where(ok, b_s, 0.0)
        b = a * b_s + b
        a = a * a_s
    return a, b


def _chunk_scan(a_ref, b_ref, r0, hcar, row8, reverse):
    n_tiles = SCAN_CHUNK // SUBLANES
    a = a_ref[pl.ds(r0, SCAN_CHUNK), :]
    b = b_ref[pl.ds(r0, SCAN_CHUNK), :]
    edge = 0 if reverse else SUBLANES - 1
    order = reversed(range(n_tiles)) if reverse else range(n_tiles)
    for v in order:
        at, bt = _tile_scan(a[v * SUBLANES:(v + 1) * SUBLANES], b[v * SUBLANES:(v + 1) * SUBLANES], row8, reverse)
        b_ref[pl.ds(r0 + v * SUBLANES, SUBLANES), :] = at * hcar + bt
        hcar = (jnp.broadcast_to(at[edge:edge + 1], (SUBLANES, LANES)) * hcar
                + jnp.broadcast_to(bt[edge:edge + 1], (SUBLANES, LANES)))
    return hcar


def _gelu_tanh(v):
    return 0.5 * v * (1.0 + jnp.tanh(math.sqrt(2.0 / math.pi) * (v + 0.044715 * (v * v * v))))


def _mixer_kernel(h_ref, w_ref, gw_ref, p_ref, za_ref, zb_ref, y_s, g_s, af_s, bf_s, ab_s, bb_s,
                  *, t_all, n_ctx, mm_rows):
    ch = SCAN_CHUNK
    n_sub = t_all // ch
    n_mm = t_all // mm_rows
    halo = SUBLANES
    ext = ch + 2 * halo
    zero_halo = jnp.zeros((halo, y_s.shape[1]), F32)
    y_s[pl.ds(0, halo), :] = zero_halo
    y_s[pl.ds(halo + t_all, halo), :] = zero_halo

    prm = p_ref[0]
    ca = [prm[k:k + 1] for k in range(3)]
    cb = [prm[3 + k:4 + k] for k in range(4)]
    cb_bias = prm[7:8]
    dirs = [(prm[8:9], prm[9:10], -LRU_C * jnp.log1p(jnp.exp(-prm[10:11])), af_s, bf_s),
            (prm[11:12], prm[12:13], -LRU_C * jnp.log1p(jnp.exp(-prm[13:14])), ab_s, bb_s)]
    row8 = lax.broadcasted_iota(jnp.int32, (SUBLANES, LANES), 0)

    def mm(c):
        r0 = pl.multiple_of(c * mm_rows, mm_rows)
        y_s[pl.ds(halo + r0, mm_rows), :] = _dot(h_ref[0, pl.ds(r0, mm_rows), :], w_ref[0])

    def shifted(v_ext, back):
        return pltpu.roll(v_ext, back % ext, 0)[halo:halo + ch]

    def stage_a(e):
        r0 = e * ch if isinstance(e, int) else pl.multiple_of(e * ch, ch)
        rows = r0 + lax.broadcasted_iota(jnp.int32, (ch, LANES), 0)
        first = (rows == 0) | (rows == n_ctx)
        second = (rows == 1) | (rows == n_ctx + 1)
        last = (rows == n_ctx - 1) | (rows == t_all - 1)
        prod = y_s[pl.ds(r0, ext), LANES:2 * LANES] * y_s[pl.ds(r0, ext), 2 * LANES:3 * LANES]
        conv = (ca[0] * jnp.where(first, 0.0, shifted(prod, 1))
                + ca[1] * prod[halo:halo + ch]
                + ca[2] * jnp.where(last, 0.0, shifted(prod, -1)))
        za_ref[0, pl.ds(r0, ch), :] = (y_s[pl.ds(halo + r0, ch), 0:LANES] * conv).astype(za_ref.dtype)
        g_s[pl.ds(r0, ch), :] = _gelu_tanh(y_s[pl.ds(halo + r0, ch), 3 * LANES:4 * LANES])
        xb = y_s[pl.ds(r0, ext), 4 * LANES:5 * LANES]
        u = (cb[0] * jnp.where(first | second, 0.0, shifted(xb, 2))
             + cb[1] * jnp.where(first, 0.0, shifted(xb, 1))
             + cb[2] * xb[halo:halo + ch]
             + cb[3] * jnp.where(last, 0.0, shifted(xb, -1))
             + cb_bias)
        gates = _dot(u.astype(BF16), gw_ref[0])
        for k, (ba, bi, cl, a_s, b_s) in enumerate(dirs):
            r = _sigmoid(gates[:, 2 * k * LANES:(2 * k + 1) * LANES] + ba)
            ig = _sigmoid(gates[:, (2 * k + 1) * LANES:(2 * k + 2) * LANES] + bi)
            a = jnp.exp(cl * r)
            a_s[pl.ds(r0, ch), :] = a
            b_s[pl.ds(r0, ch), :] = jnp.sqrt(1.0 - a * a) * (ig * u)

    per = mm_rows // ch

    def warmup(c, carry):
        mm(c)
        return carry

    lax.fori_loop(0, 2, warmup, 0)
    stage_a(0)

    def pipelined(k, carry):
        stage_a(per * k - 3)
        stage_a(per * k - 2)
        mm(k)
        return carry

    lax.fori_loop(2, n_mm, pipelined, 0)

    def tail(e, carry):
        stage_a(e)
        return carry

    lax.fori_loop(n_sub - 3, n_sub, tail, 0)

    n_ctx_chunk = n_ctx // ch

    def scans(k, carry):
        hf, hb = carry
        hf = _chunk_scan(af_s, bf_s, pl.multiple_of(k * ch, ch), hf, row8, False)
        c = jnp.where(k < n_ctx_chunk, n_ctx_chunk - 1 - k, n_sub - 1 - (k - n_ctx_chunk))
        hb = _chunk_scan(ab_s, bb_s, pl.multiple_of(c * ch, ch), hb, row8, True)
        return hf, hb

    zero_state = jnp.zeros((SUBLANES, LANES), F32)
    lax.fori_loop(0, n_sub, scans, (zero_state, zero_state))

    def combine(e, carry):
        r0 = pl.multiple_of(e * ch, ch)
        zb_ref[0, pl.ds(r0, ch), :] = (g_s[pl.ds(r0, ch), :]
                                       * (bf_s[pl.ds(r0, ch), :] + bb_s[pl.ds(r0, ch), :])).astype(zb_ref.dtype)
        return carry

    lax.fori_loop(0, n_sub, combine, 0)


def _mixer0(h, w_in, conv_a_w, conv_b_w, conv_b_b, lru_wa, lru_ba, lru_wi, lru_bi, lru_lam, n_ctx):
    bsz, t_all, d = h.shape
    width = conv_a_w.shape[-1]
    heads = width // LANES
    assert lru_wa.shape[1] == heads and lru_wa.shape[2] == LANES
    mm_rows = 2 * SCAN_CHUNK
    assert t_all % mm_rows == 0 and t_all // mm_rows >= 2
    w_r = w_in.reshape(d, 5, heads, LANES).transpose(2, 0, 1, 3).reshape(heads, d, 5 * LANES).astype(BF16)
    gw = jnp.concatenate([lru_wa[0], lru_wi[0], lru_wa[1], lru_wi[1]], axis=-1).astype(BF16)

    def per_head(v):
        return v.reshape(-1, heads, LANES).transpose(1, 0, 2)

    prm = jnp.concatenate([
        per_head(conv_a_w), per_head(conv_b_w), per_head(conv_b_b[None]),
        per_head(lru_ba[0:1]), per_head(lru_bi[0:1]), per_head(lru_lam[0:1]),
        per_head(lru_ba[1:2]), per_head(lru_bi[1:2]), per_head(lru_lam[1:2]),
        jnp.zeros((heads, 2, LANES), F32)], axis=1)
    out_sds = jax.ShapeDtypeStruct((bsz, t_all, width), BF16)
    out_spec = pl.BlockSpec((1, t_all, LANES), lambda b, j: (b, 0, j))
    seq_scratch = pltpu.VMEM((t_all, LANES), F32)
    return pl.pallas_call(
        functools.partial(_mixer_kernel, t_all=t_all, n_ctx=n_ctx, mm_rows=mm_rows),
        grid=(bsz, heads),
        in_specs=[pl.BlockSpec((1, t_all, d), lambda b, j: (b, 0, 0)),
                  pl.BlockSpec((1, d, 5 * LANES), lambda b, j: (j, 0, 0)),
                  pl.BlockSpec((1, LANES, 4 * LANES), lambda b, j: (j, 0, 0)),
                  pl.BlockSpec((1, 16, LANES), lambda b, j: (j, 0, 0))],
        out_specs=[out_spec, out_spec],
        out_shape=[out_sds, out_sds],
        scratch_shapes=[pltpu.VMEM((t_all + 2 * SUBLANES, 5 * LANES), F32),
                        seq_scratch, seq_scratch, seq_scratch, seq_scratch, seq_scratch],
        compiler_params=_cparams("arbitrary", "arbitrary"),
        name="mixer0_conv_lru",
    )(h, w_r, gw, prm)


def _split_router_weights(rw_ref, rw2_ref):
    n_exp = N_GROUPS * EXPERTS_PER_GROUP
    rw = rw_ref[...]
    rw_hi = rw.astype(BF16).astype(F32)
    rw2_ref[...] = (rw_hi + pltpu.roll(rw - rw_hi, n_exp, 1)).astype(rw2_ref.dtype)


def _route(f, rw2_ref, rb_ref, carry_ref, info_ref, meta_ref, cnt_ref, tm):
    n_exp = N_GROUPS * EXPERTS_PER_GROUP
    f_hi = f.astype(BF16)
    f_lo = (f - f_hi.astype(F32)).astype(BF16)
    rw2 = rw2_ref[...]
    lt = (_dot(f_hi, rw2) + _dot(f_lo, rw2)).T
    scores = _sigmoid(lt[0:n_exp] + lt[n_exp:2 * n_exp])
    sel = scores + rb_ref[...]
    s_rows = [scores[e:e + 1] for e in range(n_exp)]
    v_rows = [sel[e:e + 1] for e in range(n_exp)]
    best = None
    gidx = None
    for g in range(N_GROUPS):
        a, b, c, d = v_rows[4 * g:4 * g + 4]
        hi1, lo1 = jnp.maximum(a, b), jnp.minimum(a, b)
        hi2, lo2 = jnp.maximum(c, d), jnp.minimum(c, d)
        gs = jnp.maximum(hi1, hi2) + jnp.maximum(jnp.minimum(hi1, hi2), jnp.maximum(lo1, lo2))
        if g == 0:
            best, gidx = gs, jnp.zeros(gs.shape, jnp.int32)
        else:
            better = gs > best
            gidx = jnp.where(better, g, gidx)
            best = jnp.where(better, gs, best)
    found = None
    for e in range(n_exp):
        g = e // EXPERTS_PER_GROUP
        rank = jnp.zeros(best.shape, jnp.int32)
        for e2 in range(4 * g, 4 * g + 4):
            if e2 == e:
                continue
            ahead = v_rows[e2] > v_rows[e]
            if e2 < e:
                ahead = ahead | (v_rows[e2] == v_rows[e])
            rank = rank + ahead.astype(jnp.int32)
        chosen = (gidx == g) & (rank < 2)
        if e == 0:
            found = chosen
            lo_idx = jnp.zeros(best.shape, jnp.int32)
            hi_idx = jnp.zeros(best.shape, jnp.int32)
            lo_s = s_rows[0]
            hi_s = s_rows[0]
        else:
            take_lo = chosen & jnp.logical_not(found)
            lo_idx = jnp.where(take_lo, e, lo_idx)
            lo_s = jnp.where(take_lo, s_rows[e], lo_s)
            hi_idx = jnp.where(chosen, e, hi_idx)
            hi_s = jnp.where(chosen, s_rows[e], hi_s)
            found = found | chosen
    denom = lo_s + hi_s
    gate_lo = lo_s / denom
    gate_hi = hi_s / denom
    li = lo_idx - 4 * gidx
    hj = hi_idx - 4 * gidx
    pair = jnp.where(li == 0, 0, jnp.where(li == 1, 3, 5)) + hj - li - 1
    cls = PAIRS_PER_GROUP * gidx + pair
    n_cls_pad = 32
    onehot = (lax.broadcasted_iota(jnp.int32, (n_cls_pad, tm), 0) == cls).astype(F32)
    tri = (lax.broadcasted_iota(jnp.int32, (tm, tm), 0)
           < lax.broadcasted_iota(jnp.int32, (tm, tm), 1)).astype(BF16)
    excl = _dot(onehot.astype(BF16), tri)
    rank_in_cls = jnp.sum(onehot * (excl + carry_ref[:, 0:1]), axis=0, keepdims=True)
    carry_ref[...] = carry_ref[...] + jnp.sum(onehot, axis=1, keepdims=True)
    cnt_ref[...] = carry_ref[...]
    cls_f = cls.astype(F32)
    rid = lax.broadcasted_iota(jnp.int32, (LANES, tm), 0)
    packed = jnp.where(rid == 1, gate_lo, jnp.where(rid == 2, gate_hi, 0.0))
    info_ref[...] = packed.T
    rid8 = lax.broadcasted_iota(jnp.int32, (SUBLANES, tm), 0)
    meta_ref[0] = jnp.where(rid8 == 0, cls_f, jnp.where(rid8 == 1, rank_in_cls, 0.0))


def _post_mixer(xin_cols, y_cols, mod_ref, g_ref, rw_ref, rb_ref, outs, scratch, tm):
    x1_ref, f_ref, info_ref, meta_ref, cnt_ref = outs
    carry_ref, rw2_ref = scratch
    first = (pl.program_id(0) == 0) & (pl.program_id(1) == 0)

    @pl.when(first)
    def _():
        carry_ref[...] = jnp.zeros(carry_ref.shape, F32)
        _split_router_weights(rw_ref, rw2_ref)

    mod = mod_ref[0, 0]
    d = mod.shape[-1]
    group = min(PROJ_COL_GROUP, d)
    ssq = None
    for g in range(d // group):
        cols = slice(g * group, (g + 1) * group)
        x1g = xin_cols(cols) + mod[2:3, cols] * y_cols(cols)
        x1_ref[0, :, cols] = x1g
        part = jnp.sum(x1g * x1g, axis=-1, keepdims=True)
        ssq = part if ssq is None else ssq + part
    f = (x1_ref[0] * lax.rsqrt(ssq * (1.0 / d) + EPS) * g_ref[...]) * (1.0 + mod[4:5]) + mod[3:4]
    f_ref[0] = f
    _route(f, rw2_ref, rb_ref, carry_ref, info_ref, meta_ref, cnt_ref, tm)


def _outproj0_kernel(za_ref, zb_ref, wa_ref, wb_ref, x_ref, ctx_ref, mod_ref, g_ref, rw_ref, rb_ref,
                     x1_ref, f_ref, info_ref, meta_ref, cnt_ref, carry_ref, rw2_ref, *, n_ctx_tiles, tm):
    is_ctx = pl.program_id(1) < n_ctx_tiles
    za, zb = za_ref[0], zb_ref[0]
    _post_mixer(lambda cols: jnp.where(is_ctx, ctx_ref[0, :, cols], x_ref[0, :, cols]),
                lambda cols: _dot(za, wa_ref[:, cols]) + _dot(zb, wb_ref[:, cols]),
                mod_ref, g_ref, rw_ref, rb_ref, (x1_ref, f_ref, info_ref, meta_ref, cnt_ref),
                (carry_ref, rw2_ref), tm)


def _outproj1_kernel(z_ref, w_ref, x_ref, mod_ref, g_ref, rw_ref, rb_ref,
                     x1_ref, f_ref, info_ref, meta_ref, cnt_ref, carry_ref, rw2_ref, *, tm):
    z = z_ref[0]
    _post_mixer(lambda cols: x_ref[0, :, cols], lambda cols: _dot(z, w_ref[:, cols]),
                mod_ref, g_ref, rw_ref, rb_ref, (x1_ref, f_ref, info_ref, meta_ref, cnt_ref),
                (carry_ref, rw2_ref), tm)


def _router_operands(router_w, router_b):
    d, n_exp = router_w.shape
    rw = jnp.zeros((d, LANES), F32).at[:, :n_exp].set(router_w)
    return rw, router_b.reshape(n_exp, 1)


def _post_outs(bsz, tiles, tm, d):
    n_tok = bsz * tiles * tm
    out_specs = [pl.BlockSpec((1, tm, d), lambda b, i: (b, i, 0)),
                 pl.BlockSpec((1, tm, d), lambda b, i: (b, i, 0)),
                 pl.BlockSpec((tm, LANES), lambda b, i: (b * tiles + i, 0)),
                 pl.BlockSpec((1, SUBLANES, tm), lambda b, i: (b * tiles + i, 0, 0)),
                 pl.BlockSpec((32, LANES), lambda b, i: (0, 0))]
    out_shape = [jax.ShapeDtypeStruct((bsz, tiles * tm, d), F32),
                 jax.ShapeDtypeStruct((bsz, tiles * tm, d), F32),
                 jax.ShapeDtypeStruct((n_tok, LANES), F32),
                 jax.ShapeDtypeStruct((bsz * tiles, SUBLANES, tm), F32),
                 jax.ShapeDtypeStruct((32, LANES), F32)]
    return out_specs, out_shape


def _outproj0(za, zb, w_out, x, ctx, mod, g_ffn, router_w, router_b, layer):
    bsz, t_all, width = za.shape
    d = x.shape[-1]
    tm = TOKEN_TILE
    nct = ctx.shape[1] // tm
    tiles = t_all // tm
    w_bf = w_out.astype(BF16)
    rw, rb = _router_operands(router_w, router_b)
    x_spec, c_spec = _dual_specs(bsz, tm, d, nct)
    z_spec = pl.BlockSpec((1, tm, width), lambda b, i: (b, i, 0))
    out_specs, out_shape = _post_outs(bsz, tiles, tm, d)
    return pl.pallas_call(
        functools.partial(_outproj0_kernel, n_ctx_tiles=nct, tm=tm),
        grid=(bsz, tiles),
        in_specs=[z_spec, z_spec,
                  pl.BlockSpec((width, d), lambda b, i: (0, 0)),
                  pl.BlockSpec((width, d), lambda b, i: (1, 0)),
                  x_spec, c_spec, _mod_spec(layer, bsz, nct, d),
                  pl.BlockSpec((1, d), lambda b, i: (0, 0)),
                  pl.BlockSpec((d, LANES), lambda b, i: (0, 0)),
                  pl.BlockSpec(rb.shape, lambda b, i: (0, 0))],
        out_specs=out_specs,
        out_shape=out_shape,
        scratch_shapes=[pltpu.VMEM((32, LANES), F32), pltpu.VMEM((d, LANES), BF16)],
        compiler_params=_cparams("arbitrary", "arbitrary"),
        name="outproj0_router",
    )(za, zb, w_bf, w_bf, x, ctx, mod, g_ffn[layer:layer + 1], rw, rb)


def _outproj1(z, w_out, x_all, mod, g_ffn, router_w, router_b, layer, n_ctx):
    bsz, n_lat, width = z.shape
    d = x_all.shape[-1]
    tm = TOKEN_TILE
    nct = n_ctx // tm
    tiles = n_lat // tm
    rw, rb = _router_operands(router_w, router_b)
    out_specs, out_shape = _post_outs(bsz, tiles, tm, d)
    return pl.pallas_call(
        functools.partial(_outproj1_kernel, tm=tm),
        grid=(bsz, tiles),
        in_specs=[pl.BlockSpec((1, tm, width), lambda b, i: (b, i, 0)),
                  pl.BlockSpec((width, d), lambda b, i: (0, 0)),
                  pl.BlockSpec((1, tm, d), lambda b, i: (b, i + nct, 0)),
                  pl.BlockSpec((1, 1, 6, d), lambda b, i: (layer, b, 0, 0)),
                  pl.BlockSpec((1, d), lambda b, i: (0, 0)),
                  pl.BlockSpec((d, LANES), lambda b, i: (0, 0)),
                  pl.BlockSpec(rb.shape, lambda b, i: (0, 0))],
        out_specs=out_specs,
        out_shape=out_shape,
        scratch_shapes=[pltpu.VMEM((32, LANES), F32), pltpu.VMEM((d, LANES), BF16)],
        compiler_params=_cparams("arbitrary", "arbitrary"),
        name="outproj1_router",
    )(z, w_out.astype(BF16), x_all, mod, g_ffn[layer:layer + 1], rw, rb)


def _dispatch_plan(meta, counts, n_tok, tme):
    cls = meta[:, 0, :].reshape(-1).astype(jnp.int32)
    rank = meta[:, 1, :].reshape(-1).astype(jnp.int32)
    cnt = counts[:N_CLASSES, 0].astype(jnp.int32)
    padded = ((cnt + tme - 1) // tme) * tme
    ends = jnp.cumsum(padded)
    starts = ends - padded
    pos = starts[cls] + rank
    r_pad = n_tok + N_CLASSES * tme
    n_tiles = r_pad // tme
    cls_tiles = padded // tme
    cls_tile_start = starts // tme
    used_tiles = ends[-1] // tme
    spare = used_tiles + jnp.arange(N_CLASSES, dtype=jnp.int32)
    fill_tile = jnp.concatenate([ends // tme - 1, spare])
    fill_ok = jnp.concatenate([cnt > 0, spare < n_tiles])
    fill = (jnp.where(fill_ok, fill_tile, 0).astype(jnp.int32), fill_ok.astype(jnp.int32))
    seg_cls = np.array([[c for c in range(N_CLASSES) if _CLASS_LO[c] == e or _CLASS_HI[c] == e]
                        for e in range(N_GROUPS * EXPERTS_PER_GROUP)], np.int32).reshape(-1)
    seg_exp = np.repeat(np.arange(N_GROUPS * EXPERTS_PER_GROUP, dtype=np.int32), 3)
    seg_slot = (_CLASS_HI[seg_cls] == seg_exp).astype(np.int32)
    seg_len = cls_tiles[seg_cls]
    seg_end = jnp.cumsum(seg_len)
    seg_start = seg_end - seg_len
    total = seg_end[-1]
    n_steps = 2 * n_tiles
    q = jnp.arange(n_steps, dtype=jnp.int32)
    qc = jnp.minimum(q, total - 1)
    seg = jnp.sum((seg_end[None, :] <= qc[:, None]).astype(jnp.int32), axis=1)
    real = q < total
    extra = q - total
    step_tile = jnp.where(real, cls_tile_start[seg_cls][seg] + (qc - seg_start[seg]), total // 2 + extra // 2)
    step_exp = jnp.asarray(seg_exp)[seg]
    step_slot = jnp.where(real, jnp.asarray(seg_slot)[seg], extra % 2)
    step_valid = real.astype(jnp.int32)
    n_exp = N_GROUPS * EXPERTS_PER_GROUP
    has_tiles = jnp.sum(seg_len.reshape(n_exp, 3), axis=1) > 0
    ids = jnp.arange(n_exp, dtype=jnp.int32)
    later = jnp.where(has_tiles[None, :] & (ids[None, :] > ids[:, None]), ids[None, :], n_exp)
    next_exp = jnp.min(later, axis=1)
    next_exp = jnp.where(next_exp == n_exp, -1, next_exp).astype(jnp.int32)
    prev_exp = jnp.concatenate([jnp.full((1,), -1, jnp.int32), step_exp[:-1]])
    step_first = (real & (step_exp != prev_exp)).astype(jnp.int32)
    step_next = next_exp[step_exp]
    return pos, fill, r_pad, (step_exp, step_tile, step_slot, step_valid, step_first, step_next)


def _gather_rows(idx_ref, nxt_ref, src_hbm, buf, sem, rows, step, n_steps):
    def issue(ref, slot):
        def body(g, carry):
            for u in range(DMA_ISSUE_UNROLL):
                r = g * DMA_ISSUE_UNROLL + u
                pltpu.make_async_copy(src_hbm.at[pl.ds(ref[0, 0, r], 1), :], buf.at[slot, pl.ds(r, 1), :],
                                      sem.at[slot]).start(priority=u % 2)
            return carry
        lax.fori_loop(0, rows // DMA_ISSUE_UNROLL, body, 0)

    slot = step % 2

    @pl.when(step == 0)
    def _():
        issue(idx_ref, 0)

    @pl.when(step + 1 < n_steps)
    def _():
        issue(nxt_ref, 1 - slot)

    pltpu.make_async_copy(src_hbm.at[pl.ds(0, rows), :], buf.at[slot], sem.at[slot]).wait()
    return slot


def _scatter_sorted_kernel(fill_tile_ref, fill_ok_ref, pos_ref, f_ref, xs_hbm, zero_s, sem_z, sem,
                           *, rows, tme, n_fill):
    @pl.when(pl.program_id(0) == 0)
    def _():
        zero_s[...] = jnp.zeros(zero_s.shape, zero_s.dtype)

        def fill(k):
            r0 = pl.multiple_of(fill_tile_ref[k] * tme, tme)
            return pltpu.make_async_copy(zero_s, xs_hbm.at[pl.ds(r0, tme), :], sem_z)

        def start(k, carry):
            @pl.when(fill_ok_ref[k] == 1)
            def _():
                fill(k).start()
            return carry

        def wait(k, carry):
            @pl.when(fill_ok_ref[k] == 1)
            def _():
                fill(k).wait()
            return carry

        lax.fori_loop(0, n_fill, start, 0)
        lax.fori_loop(0, n_fill, wait, 0)

    def body(g, carry):
        for u in range(DMA_ISSUE_UNROLL):
            r = g * DMA_ISSUE_UNROLL + u
            pltpu.make_async_copy(f_ref.at[pl.ds(r, 1), :], xs_hbm.at[pl.ds(pos_ref[0, 0, r], 1), :],
                                  sem).start(priority=u % 2)
        return carry

    lax.fori_loop(0, rows // DMA_ISSUE_UNROLL, body, 0)
    pltpu.make_async_copy(f_ref, xs_hbm.at[pl.ds(0, rows), :], sem).wait()


def _scatter_sorted(f_flat, pos, fill, r_pad, tme):
    n_tok, d = f_flat.shape
    tm = TOKEN_TILE
    n_steps = n_tok // tm
    fill_tile, fill_ok = fill
    grid_spec = pltpu.PrefetchScalarGridSpec(
        num_scalar_prefetch=2,
        grid=(n_steps,),
        in_specs=[pl.BlockSpec((1, 1, tm), lambda i, ft, fo: (i, 0, 0), memory_space=pltpu.SMEM),
                  pl.BlockSpec((tm, d), lambda i, ft, fo: (i, 0))],
        out_specs=pl.BlockSpec(memory_space=pl.ANY),
        scratch_shapes=[pltpu.VMEM((tme, d), F32), pltpu.SemaphoreType.DMA, pltpu.SemaphoreType.DMA],
    )
    return pl.pallas_call(
        functools.partial(_scatter_sorted_kernel, rows=tm, tme=tme, n_fill=fill_tile.shape[0]),
        grid_spec=grid_spec,
        out_shape=jax.ShapeDtypeStruct((r_pad, d), F32),
        compiler_params=_cparams("arbitrary"),
        name="moe_scatter_sorted",
    )(fill_tile, fill_ok, pos.reshape(n_steps, 1, tm), f_flat)


def _expert_kernel(exp_ref, tile_ref, slot_ref, valid_ref, first_ref, next_ref, x_ref, wg_hbm, wu_hbm, wd_hbm,
                   o_ref, wg_s, wu_s, wd_s, stage_g, stage_u, stage_d, sem, *, layer):
    s = pl.program_id(0)

    def weight_copies(e):
        return (pltpu.make_async_copy(wg_hbm.at[layer, e], stage_g, sem.at[0]),
                pltpu.make_async_copy(wu_hbm.at[layer, e], stage_u, sem.at[1]),
                pltpu.make_async_copy(wd_hbm.at[layer, e], stage_d, sem.at[2]))

    def cast(stage, dst):
        def body(i, carry):
            r0 = pl.multiple_of(i * EXPERT_CAST_ROWS, EXPERT_CAST_ROWS)
            dst[pl.ds(r0, EXPERT_CAST_ROWS), :] = stage[pl.ds(r0, EXPERT_CAST_ROWS), :].astype(dst.dtype)
            return carry
        lax.fori_loop(0, stage.shape[0] // EXPERT_CAST_ROWS, body, 0)

    @pl.when(first_ref[s] == 1)
    def _():
        @pl.when(s == 0)
        def _():
            for cp in weight_copies(exp_ref[s]):
                cp.start()

        for cp in weight_copies(exp_ref[s]):
            cp.wait()
        cast(stage_g, wg_s)
        cast(stage_u, wu_s)
        cast(stage_d, wd_s)

        @pl.when(next_ref[s] >= 0)
        def _():
            for cp in weight_copies(next_ref[s]):
                cp.start()

    @pl.when(valid_ref[s] == 1)
    def _():
        xv = x_ref[...].astype(BF16)
        hg = _dot(xv, wg_s[...])
        hu = _dot(xv, wu_s[...])
        hid = (hg * _sigmoid(hg) * hu).astype(BF16)
        o_ref[...] = _dot(hid, wd_s[...])

    @pl.when(valid_ref[s] == 0)
    def _():
        o_ref[...] = jnp.zeros(o_ref.shape, o_ref.dtype)


def _experts(xs, plan, w_gate, w_up, w_down, layer, tme):
    r_pad, d = xs.shape
    ff = w_gate.shape[-1]
    assert d % EXPERT_CAST_ROWS == 0 and ff % EXPERT_CAST_ROWS == 0
    n_steps = plan[0].shape[0]
    any_spec = pl.BlockSpec(memory_space=pl.ANY)
    grid_spec = pltpu.PrefetchScalarGridSpec(
        num_scalar_prefetch=6,
        grid=(n_steps,),
        in_specs=[pl.BlockSpec((tme, d), lambda s, e, t, w, v, f, n: (t[s], 0)), any_spec, any_spec, any_spec],
        out_specs=pl.BlockSpec((tme, d), lambda s, e, t, w, v, f, n: (t[s], w[s])),
        scratch_shapes=[pltpu.VMEM((d, ff), BF16), pltpu.VMEM((d, ff), BF16), pltpu.VMEM((ff, d), BF16),
                        pltpu.VMEM((d, ff), F32), pltpu.VMEM((d, ff), F32), pltpu.VMEM((ff, d), F32),
                        pltpu.SemaphoreType.DMA((3,))],
    )
    return pl.pallas_call(
        functools.partial(_expert_kernel, layer=layer),
        grid_spec=grid_spec,
        out_shape=jax.ShapeDtypeStruct((r_pad, 2 * d), F32),
        compiler_params=_cparams("arbitrary"),
        name="moe_experts",
    )(*plan, xs, w_gate, w_up, w_down)


def _moe_residual(rows, info_ref, x1_ref, mod_ref, d):
    info = info_ref[...]
    y = info[:, 1:2] * rows[:, 0:d] + info[:, 2:3] * rows[:, d:2 * d]
    return x1_ref[0] + mod_ref[0, 0][5:6] * y


def _linear_step(tiles):
    return pl.program_id(0) * tiles + pl.program_id(1)


def _combine0_kernel(idx_ref, nxt_ref, o_hbm, info_ref, x1_ref, mod_ref, modn_ref, g_ref, x2_ref, h_ref,
                     buf, sem, *, rows, d, tiles, n_steps):
    slot = _gather_rows(idx_ref, nxt_ref, o_hbm, buf, sem, rows, _linear_step(tiles), n_steps)
    x2 = _moe_residual(buf[slot], info_ref, x1_ref, mod_ref, d)
    x2_ref[0] = x2
    modn = modn_ref[0, 0]
    h_ref[0] = _mod_norm(x2, g_ref[...], modn[1:2], modn[0:1]).astype(h_ref.dtype)


def _combine1_kernel(idx_ref, nxt_ref, o_hbm, info_ref, x1_ref, mod_ref, g_ref, out_ref, buf, sem,
                     *, rows, d, tiles, n_steps):
    slot = _gather_rows(idx_ref, nxt_ref, o_hbm, buf, sem, rows, _linear_step(tiles), n_steps)
    x2 = _moe_residual(buf[slot], info_ref, x1_ref, mod_ref, d)
    ms = jnp.mean(x2 * x2, axis=-1, keepdims=True)
    out_ref[0] = x2 * lax.rsqrt(ms + EPS) * g_ref[...]


def _pos_specs(tiles, n_steps, tm):
    cur = pl.BlockSpec((1, 1, tm), lambda b, i: (b * tiles + i, 0, 0), memory_space=pltpu.SMEM)
    nxt = pl.BlockSpec((1, 1, tm), lambda b, i: (jnp.minimum(b * tiles + i + 1, n_steps - 1), 0, 0),
                       memory_space=pltpu.SMEM)
    return cur, nxt


def _combine0(o_sorted, pos, info, x1, mod, norm_mix_g, layer, n_ctx):
    bsz, t_all, d = x1.shape
    tm = TOKEN_TILE
    tiles = t_all // tm
    nct = n_ctx // tm
    n_steps = bsz * tiles
    tok_spec = pl.BlockSpec((1, tm, d), lambda b, i: (b, i, 0))
    cur, nxt = _pos_specs(tiles, n_steps, tm)
    idx = pos.reshape(n_steps, 1, tm)
    return pl.pallas_call(
        functools.partial(_combine0_kernel, rows=tm, d=d, tiles=tiles, n_steps=n_steps),
        grid=(bsz, tiles),
        in_specs=[cur, nxt,
                  pl.BlockSpec(memory_space=pl.ANY),
                  pl.BlockSpec((tm, LANES), lambda b, i: (b * tiles + i, 0)),
                  tok_spec,
                  _mod_spec(layer, bsz, nct, d),
                  _mod_spec(layer + 1, bsz, nct, d),
                  pl.BlockSpec((1, d), lambda b, i: (0, 0))],
        out_specs=[tok_spec, tok_spec],
        out_shape=[jax.ShapeDtypeStruct((bsz, t_all, d), F32), jax.ShapeDtypeStruct((bsz, t_all, d), BF16)],
        scratch_shapes=[pltpu.VMEM((2, tm, 2 * d), F32), pltpu.SemaphoreType.DMA((2,))],
        compiler_params=_cparams("arbitrary", "arbitrary"),
        name="moe_combine0",
    )(idx, idx, o_sorted, info, x1, mod, mod, norm_mix_g[layer + 1:layer + 2])


def _combine1(o_sorted, pos, info, x1, mod, final_g, layer):
    bsz, n_lat, d = x1.shape
    tm = TOKEN_TILE
    tiles = n_lat // tm
    n_steps = bsz * tiles
    tok_spec = pl.BlockSpec((1, tm, d), lambda b, i: (b, i, 0))
    cur, nxt = _pos_specs(tiles, n_steps, tm)
    idx = pos.reshape(n_steps, 1, tm)
    return pl.pallas_call(
        functools.partial(_combine1_kernel, rows=tm, d=d, tiles=tiles, n_steps=n_steps),
        grid=(bsz, tiles),
        in_specs=[cur, nxt,
                  pl.BlockSpec(memory_space=pl.ANY),
                  pl.BlockSpec((tm, LANES), lambda b, i: (b * tiles + i, 0)),
                  tok_spec,
                  pl.BlockSpec((1, 1, 6, d), lambda b, i: (layer, b, 0, 0)),
                  pl.BlockSpec((1, d), lambda b, i: (0, 0))],
        out_specs=tok_spec,
        out_shape=jax.ShapeDtypeStruct((bsz, n_lat, d), F32),
        scratch_shapes=[pltpu.VMEM((2, tm, 2 * d), F32), pltpu.SemaphoreType.DMA((2,))],
        compiler_params=_cparams("arbitrary", "arbitrary"),
        name="moe_combine1",
    )(idx, idx, o_sorted, info, x1, mod, final_g.reshape(1, d))


def _moe_sorted_outputs(f, meta, counts, expert_w, layer):
    d = f.shape[-1]
    f_flat = f.reshape(-1, d)
    n_tok = f_flat.shape[0]
    tme = EXPERT_TILE
    pos, fill, r_pad, plan = _dispatch_plan(meta, counts, n_tok, tme)
    xs = _scatter_sorted(f_flat, pos, fill, r_pad, tme)
    return _experts(xs, plan, *expert_w, layer, tme), pos


def _rope_tables(n_ctx, n_lat, qk_dim):
    axis_dim = qk_dim // 2
    half = axis_dim // 2
    rows = n_lat // GRID_W
    row = jnp.repeat(jnp.arange(rows, dtype=F32), GRID_W)
    col = jnp.tile(jnp.arange(GRID_W, dtype=F32), rows)
    inv = ROPE_BASE ** (-jnp.arange(0, axis_dim, 2, dtype=F32) / axis_dim)
    ang_r = row[:, None] * inv
    ang_c = col[:, None] * inv
    ang = jnp.concatenate([ang_r, ang_r, ang_c, ang_c], axis=-1)
    cos, sin = jnp.cos(ang), jnp.sin(ang)
    lower = (jnp.arange(qk_dim) % axis_dim) < half
    sin_up = jnp.where(lower, -sin, 0.0)
    sin_dn = jnp.where(lower, 0.0, sin)
    pad = jnp.zeros((n_ctx, qk_dim), F32)
    return (jnp.concatenate([pad + 1.0, cos], axis=0), jnp.concatenate([pad, sin_up], axis=0),
            jnp.concatenate([pad, sin_dn], axis=0))


def _proj_kernel(*refs, qk_dim, mult, rope, group):
    if rope:
        h_ref, w_ref, cos_ref, su_ref, sd_ref, o_ref = refs
        cos, su, sd = cos_ref[...], su_ref[...], sd_ref[...]
        half = qk_dim // 4
    else:
        h_ref, w_ref, o_ref = refs
    h = h_ref[0]
    for g in range(w_ref.shape[1] // group):
        y = _dot(h, w_ref[:, g * group:(g + 1) * group])
        if not rope:
            o_ref[0, :, g * group:(g + 1) * group] = y.astype(o_ref.dtype)
            continue
        for k in range(group // qk_dim):
            t = y[:, k * qk_dim:(k + 1) * qk_dim]
            up = pltpu.roll(t, qk_dim - half, 1)
            dn = pltpu.roll(t, half, 1)
            r = t * cos + up * su + dn * sd
            if mult != 1.0:
                r = r * mult
            c0 = g * group + k * qk_dim
            o_ref[0, :, c0:c0 + qk_dim] = r.astype(o_ref.dtype)


def _projection(h, w_bf, col_block, row_off, n_rows, tables, mult, qk_dim, name):
    bsz, _, d = h.shape
    cols = w_bf.shape[1] // 3
    tm = _pick_tile(math.gcd(n_rows, row_off) if row_off else n_rows, PROJ_TILES)
    off = row_off // tm
    tab = pl.BlockSpec((tm, qk_dim), lambda b, i: (i, 0))
    in_specs = [pl.BlockSpec((1, tm, d), lambda b, i: (b, i + off, 0)),
                pl.BlockSpec((d, cols), lambda b, i: (0, col_block))]
    operands = [h, w_bf]
    if tables is not None:
        in_specs += [tab, tab, tab]
        operands += list(tables)
    return pl.pallas_call(
        functools.partial(_proj_kernel, qk_dim=qk_dim, mult=mult, rope=tables is not None,
                          group=min(PROJ_COL_GROUP, cols)),
        grid=(bsz, n_rows // tm),
        in_specs=in_specs,
        out_specs=pl.BlockSpec((1, tm, cols), lambda b, i: (b, i, 0)),
        out_shape=jax.ShapeDtypeStruct((bsz, n_rows, cols), BF16),
        compiler_params=_cparams("arbitrary", "arbitrary"),
        name=name,
    )(*operands)


def _qkv(h, w_in_o, n_ctx, qk_dim):
    t_all = h.shape[1]
    n_lat = t_all - n_ctx
    w_bf = w_in_o.astype(BF16)
    cos, su, sd = _rope_tables(n_ctx, n_lat, qk_dim)
    q_scale = qk_dim ** -0.5 * math.log2(math.e)
    q = _projection(h, w_bf, 0, n_ctx, n_lat, (cos[n_ctx:], su[n_ctx:], sd[n_ctx:]), q_scale, qk_dim, "q_rope")
    k = _projection(h, w_bf, 1, 0, t_all, (cos, su, sd), 1.0, qk_dim, "k_rope")
    v = _projection(h, w_bf, 2, 0, t_all, None, 1.0, qk_dim, "v_proj")
    return q, k, v


def _attn_kernel(q_ref, k_ref, v_ref, lp_ref, g_ref, o_ref, s_a, e_a, l_a, s_b, e_b, l_b,
                 *, qk_dim, lam_init, tq, kc):
    lp = lp_ref[...]
    lam = (jnp.exp(jnp.sum(lp[0:1] * lp[1:2], axis=-1, keepdims=True))
           - jnp.exp(jnp.sum(lp[2:3] * lp[3:4], axis=-1, keepdims=True)) + lam_init)
    t_all = k_ref.shape[1]
    n_kc = t_all // kc
    n_q = q_ref.shape[1] // tq
    half = kc // 2
    nt = (((1,), (1,)), ((), ()))

    def row0(i):
        return i * tq if isinstance(i, int) else pl.multiple_of(i * tq, tq)

    def stage(i_soft, soft, i_val, val):
        pv = [None, None]
        for m in range(2):
            dims = slice(m * qk_dim, (m + 1) * qk_dim)
            if soft is not None:
                s_s, e_s, l_s = soft
                qm = q_ref[0, pl.ds(row0(i_soft), tq), dims]
            mx = None
            for c in range(n_kc):
                keys = slice(c * kc, (c + 1) * kc)
                if val is not None:
                    part = _dot(val[1][m, :, keys], v_ref[0, keys, :])
                    pv[m] = part if pv[m] is None else pv[m] + part
                if soft is not None:
                    s = lax.dot_general(qm, k_ref[0, keys, dims], nt,
                                        preferred_element_type=F32)
                    s_s[m, :, keys] = s
                    cm = jnp.maximum(s[:, :half], s[:, half:])
                    mx = cm if mx is None else jnp.maximum(mx, cm)
            if soft is not None:
                row_max = jnp.max(mx, axis=-1, keepdims=True)
                acc = None
                for c in range(n_kc):
                    keys = slice(c * kc, (c + 1) * kc)
                    e = jnp.exp2(s_s[m, :, keys] - row_max)
                    e_s[m, :, keys] = e.astype(e_s.dtype)
                    part = e[:, :half] + e[:, half:]
                    acc = part if acc is None else acc + part
                l_s[m] = 1.0 / jnp.sum(acc, axis=-1, keepdims=True)
        if val is not None:
            l_v = val[2]
            o = pv[0] * l_v[0] - lam * (pv[1] * l_v[1])
            o = o * lax.rsqrt(jnp.mean(o * o, axis=-1, keepdims=True) + EPS)
            o_ref[0, pl.ds(row0(i_val), tq), :] = (o * g_ref[...] * (1.0 - lam_init)).astype(o_ref.dtype)

    set_a, set_b = (s_a, e_a, l_a), (s_b, e_b, l_b)
    stage(0, set_a, None, None)

    def pair(j, carry):
        stage(2 * j + 1, set_b, 2 * j, set_a)
        stage(2 * j + 2, set_a, 2 * j + 1, set_b)
        return carry

    lax.fori_loop(0, n_q // 2 - 1, pair, 0)
    stage(n_q - 1, set_b, n_q - 2, set_a)
    stage(None, None, n_q - 1, set_b)


def _attention(q, k, v, lam_params, subln_g, qk_dim, lam_init):
    bsz, n_lat, cols = q.shape
    t_all = k.shape[1]
    v_dim = subln_g.shape[-1]
    heads = cols // v_dim
    tq, kc = ATTN_Q_TILE, ATTN_KEY_CHUNK
    assert n_lat % (2 * tq) == 0 and t_all % kc == 0
    buffer_set = [pltpu.VMEM((2, tq, t_all), F32), pltpu.VMEM((2, tq, t_all), BF16),
                  pltpu.VMEM((2, tq, 1), F32)]
    return pl.pallas_call(
        functools.partial(_attn_kernel, qk_dim=qk_dim, lam_init=lam_init, tq=tq, kc=kc),
        grid=(bsz, heads),
        in_specs=[pl.BlockSpec((1, n_lat, v_dim), lambda b, h: (b, 0, h)),
                  pl.BlockSpec((1, t_all, v_dim), lambda b, h: (b, 0, h)),
                  pl.BlockSpec((1, t_all, v_dim), lambda b, h: (b, 0, h)),
                  pl.BlockSpec((4, qk_dim), lambda b, h: (0, 0)),
                  pl.BlockSpec((1, v_dim), lambda b, h: (0, 0))],
        out_specs=pl.BlockSpec((1, n_lat, v_dim), lambda b, h: (b, 0, h)),
        out_shape=jax.ShapeDtypeStruct((bsz, n_lat, cols), BF16),
        scratch_shapes=buffer_set + buffer_set,
        compiler_params=_cparams("arbitrary", "arbitrary"),
        name="diff_attention",
    )(q, k, v, lam_params, subln_g)


def kernel(x, c, ctx, c_ctx, ada_w, ada_b, norm_mix_g, norm_ffn_g, final_g, w_in_e, conv_a_w, conv_b_w, conv_b_b, lru_wa, lru_ba, lru_wi, lru_bi, lru_lam, w_out_e, w_in_o, lam_q1, lam_k1, lam_q2, lam_k2, subln_g, w_out_o, router_w, router_b, exp_w_gate, exp_w_up, exp_w_down):
    bsz, n_lat, d = x.shape
    n_ctx = ctx.shape[1]
    depth = ada_w.shape[0]
    assert depth == 2 and w_in_e.shape[0] == 1 and w_in_o.shape[0] == 1
    assert n_ctx % TOKEN_TILE == 0 and n_lat % TOKEN_TILE == 0 and n_ctx % SCAN_CHUNK == 0
    assert router_w.shape[1] == N_GROUPS * EXPERTS_PER_GROUP
    qk_dim = lam_q1.shape[-1]
    assert qk_dim == LANES

    cond = jnp.concatenate([c, c_ctx[None]], axis=0)
    mod = _ada(cond, ada_w, ada_b).reshape(depth, bsz + 1, 6, d)

    h0 = _prenorm(x, ctx, mod, norm_mix_g, 0)
    za, zb = _mixer0(h0, w_in_e[0], conv_a_w[0], conv_b_w[0], conv_b_b[0], lru_wa[0], lru_ba[0],
                     lru_wi[0], lru_bi[0], lru_lam[0], n_ctx)
    x1, f0, info0, meta0, cnt0 = _outproj0(za, zb, w_out_e[0], x, ctx, mod, norm_ffn_g, router_w, router_b, 0)
    expert_w = (exp_w_gate, exp_w_up, exp_w_down)
    o0, pos0 = _moe_sorted_outputs(f0, meta0, cnt0, expert_w, 0)
    x2, h1 = _combine0(o0, pos0, info0, x1, mod, norm_mix_g, 0, n_ctx)

    lam_init = 0.8 - 0.6 * math.exp(-0.3 * 1)
    q, k, v = _qkv(h1, w_in_o[0], n_ctx, qk_dim)
    lam_params = jnp.concatenate([lam_q1, lam_k1, lam_q2, lam_k2], axis=0)
    att = _attention(q, k, v, lam_params, subln_g, qk_dim, lam_init)
    x3, f1, info1, meta1, cnt1 = _outproj1(att, w_out_o[0], x2, mod, norm_ffn_g, router_w, router_b, 1, n_ctx)
    o1, pos1 = _moe_sorted_outputs(f1, meta1, cnt1, expert_w, 1)
    return _combine1(o1, pos1, info1, x3, mod, final_g, 1)
```

```python
import functools
import math

import numpy as np
import jax
import jax.numpy as jnp
from jax import lax
from jax.experimental import pallas as pl
from jax.experimental.pallas import tpu as pltpu

F32 = jnp.float32
BF16 = jnp.bfloat16

EPS = 1e-6
GRID_W = 64
LRU_C = 8.0
ROPE_BASE = 10000.0
N_GROUPS = 4
EXPERTS_PER_GROUP = 4
PAIRS_PER_GROUP = 6
N_CLASSES = N_GROUPS * PAIRS_PER_GROUP
LANES = 128
SUBLANES = 8
TOKEN_TILE = 256
EXPERT_TILE = 256
SCAN_CHUNK = 128
MIXER_HEADS_PER_STEP = 2
ATTN_Q_TILE = 256
ATTN_KEY_CHUNK = 256
PROJ_TILES = (768, 512, 256)
PROJ_COL_GROUP = 512
EXPERT_CAST_ROWS = 256
DMA_ISSUE_UNROLL = 8
VMEM_LIMIT_BYTES = 56 * 1024 * 1024

_PAIRS = [(0, 1), (0, 2), (0, 3), (1, 2), (1, 3), (2, 3)]
_CLASS_LO = np.array([4 * g + i for g in range(N_GROUPS) for (i, j) in _PAIRS], np.int32)
_CLASS_HI = np.array([4 * g + j for g in range(N_GROUPS) for (i, j) in _PAIRS], np.int32)


def _cparams(*sem):
    return pltpu.CompilerParams(dimension_semantics=sem, vmem_limit_bytes=VMEM_LIMIT_BYTES)


def _pick_tile(n, prefs):
    for t in prefs:
        if n % t == 0:
            return t
    raise ValueError(f"no tile in {prefs} divides {n}")


def _sigmoid(v):
    return 1.0 / (1.0 + jnp.exp(-v))


def _dot(a, b):
    return jnp.dot(a, b, preferred_element_type=F32)


def _mod_norm(xv, g, sc, sh):
    ms = jnp.mean(xv * xv, axis=-1, keepdims=True)
    return xv * lax.rsqrt(ms + EPS) * g * (1.0 + sc) + sh


def _ada_kernel(c_ref, w_ref, b_ref, o_ref):
    cv = c_ref[...]
    s = (cv * _sigmoid(cv)).astype(BF16)
    o_ref[0] = _dot(s, w_ref[0].astype(BF16)) + b_ref[0]


def _ada(cond, ada_w, ada_b):
    depth, d, n6 = ada_w.shape
    rows = cond.shape[0]
    tn = min(1024, n6)
    return pl.pallas_call(
        _ada_kernel,
        grid=(depth, n6 // tn),
        in_specs=[pl.BlockSpec((rows, d), lambda l, j: (0, 0)),
                  pl.BlockSpec((1, d, tn), lambda l, j: (l, 0, j)),
                  pl.BlockSpec((1, 1, tn), lambda l, j: (l, 0, j))],
        out_specs=pl.BlockSpec((1, rows, tn), lambda l, j: (l, 0, j)),
        out_shape=jax.ShapeDtypeStruct((depth, rows, n6), F32),
        compiler_params=_cparams("arbitrary", "arbitrary"),
        name="ada_modulation",
    )(cond, ada_w, ada_b.reshape(depth, 1, n6))


def _prenorm_kernel(x_ref, ctx_ref, mod_ref, g_ref, o_ref, *, n_ctx_tiles):
    i = pl.program_id(1)
    mod = mod_ref[0, 0]

    def run(src_ref):
        o_ref[0] = _mod_norm(src_ref[0], g_ref[...], mod[1:2], mod[0:1]).astype(o_ref.dtype)

    @pl.when(i < n_ctx_tiles)
    def _():
        run(ctx_ref)

    @pl.when(i >= n_ctx_tiles)
    def _():
        run(x_ref)


def _dual_specs(bsz, tm, d, nct):
    x_spec = pl.BlockSpec((1, tm, d), lambda b, i: (b, jnp.maximum(i - nct, 0), 0))
    c_spec = pl.BlockSpec((1, tm, d), lambda b, i: (b, jnp.minimum(i, nct - 1), 0))
    return x_spec, c_spec


def _mod_spec(layer, bsz, nct, d):
    return pl.BlockSpec((1, 1, 6, d), lambda b, i: (layer, jnp.where(i < nct, bsz, b), 0, 0))


def _prenorm(x, ctx, mod, g, layer):
    bsz, n_lat, d = x.shape
    n_ctx = ctx.shape[1]
    tm = TOKEN_TILE
    nct = n_ctx // tm
    t_all = n_ctx + n_lat
    x_spec, c_spec = _dual_specs(bsz, tm, d, nct)
    return pl.pallas_call(
        functools.partial(_prenorm_kernel, n_ctx_tiles=nct),
        grid=(bsz, t_all // tm),
        in_specs=[x_spec, c_spec, _mod_spec(layer, bsz, nct, d),
                  pl.BlockSpec((1, d), lambda b, i: (0, 0))],
        out_specs=pl.BlockSpec((1, tm, d), lambda b, i: (b, i, 0)),
        out_shape=jax.ShapeDtypeStruct((bsz, t_all, d), BF16),
        compiler_params=_cparams("arbitrary", "arbitrary"),
        name="prenorm0",
    )(x, ctx, mod, g[layer:layer + 1])


def _tile_scan(a, b, row, reverse):
    for s in (1, 2, 4):
        if reverse:
            a_s = pltpu.roll(a, SUBLANES - s, 0)
            b_s = pltpu.roll(b, SUBLANES - s, 0)
            ok = row < SUBLANES - s
        else:
            a_s = pltpu.roll(a, s, 0)
            b_s = pltpu.roll(b, s, 0)
            ok = row >= s
        a_s = jnp.where(ok, a_s, 1.0)
        b_s = jnp.where(ok, b_s, 0.0)
        b = a * b_s + b
        a = a * a_s
    return a, b


def _chunk_scan(a_ref, b_ref, r0, hcar, row8, reverse):
    n_tiles = SCAN_CHUNK // SUBLANES
    a = a_ref[pl.ds(r0, SCAN_CHUNK), :]
    b = b_ref[pl.ds(r0, SCAN_CHUNK), :]
    edge = 0 if reverse else SUBLANES - 1
    order = reversed(range(n_tiles)) if reverse else range(n_tiles)
    for v in order:
        at, bt = _tile_scan(a[v * SUBLANES:(v + 1) * SUBLANES], b[v * SUBLANES:(v + 1) * SUBLANES], row8, reverse)
        b_ref[pl.ds(r0 + v * SUBLANES, SUBLANES), :] = at * hcar + bt
        hcar = (jnp.broadcast_to(at[edge:edge + 1], (SUBLANES, LANES)) * hcar
                + jnp.broadcast_to(bt[edge:edge + 1], (SUBLANES, LANES)))
    return hcar


def _gelu_tanh(v):
    return 0.5 * v * (1.0 + jnp.tanh(math.sqrt(2.0 / math.pi) * (v + 0.044715 * (v * v * v))))


def _mixer_kernel(h_ref, w_ref, gw_ref, p_ref, za_ref, zb_ref, y_s, *seq, t_all, n_ctx, mm_rows, hp):
    ch = SCAN_CHUNK
    n_sub = t_all // ch
    n_mm = t_all // mm_rows
    halo = SUBLANES
    ext = ch + 2 * halo
    zero_halo = jnp.zeros((halo, y_s.shape[1]), F32)
    y_s[pl.ds(0, halo), :] = zero_halo
    y_s[pl.ds(halo + t_all, halo), :] = zero_halo
    row8 = lax.broadcasted_iota(jnp.int32, (SUBLANES, LANES), 0)
    sets = [seq[5 * q:5 * q + 5] for q in range(hp)]
    prms = []
    for q in range(hp):
        prm = p_ref[0, q]
        prms.append(([prm[k:k + 1] for k in range(3)], [prm[3 + k:4 + k] for k in range(4)], prm[7:8],
                     [(prm[8:9], prm[9:10], -LRU_C * jnp.log1p(jnp.exp(-prm[10:11]))),
                      (prm[11:12], prm[12:13], -LRU_C * jnp.log1p(jnp.exp(-prm[13:14])))]))

    def mm(c):
        r0 = pl.multiple_of(c * mm_rows, mm_rows)
        y_s[pl.ds(halo + r0, mm_rows), :] = _dot(h_ref[0, pl.ds(r0, mm_rows), :], w_ref[0])

    def shifted(v_ext, back):
        return pltpu.roll(v_ext, back % ext, 0)[halo:halo + ch]

    def stage_a(e):
        r0 = e * ch if isinstance(e, int) else pl.multiple_of(e * ch, ch)
        rows = r0 + lax.broadcasted_iota(jnp.int32, (ch, LANES), 0)
        first = (rows == 0) | (rows == n_ctx)
        second = (rows == 1) | (rows == n_ctx + 1)
        last = (rows == n_ctx - 1) | (rows == t_all - 1)
        for q in range(hp):
            ca, cb, cb_bias, dirs = prms[q]
            g_s, af_s, bf_s, ab_s, bb_s = sets[q]
            c0 = q * 5 * LANES
            prod = (y_s[pl.ds(r0, ext), c0 + LANES:c0 + 2 * LANES]
                    * y_s[pl.ds(r0, ext), c0 + 2 * LANES:c0 + 3 * LANES])
            conv = (ca[0] * jnp.where(first, 0.0, shifted(prod, 1))
                    + ca[1] * prod[halo:halo + ch]
                    + ca[2] * jnp.where(last, 0.0, shifted(prod, -1)))
            za_ref[0, pl.ds(r0, ch), q * LANES:(q + 1) * LANES] = (
                y_s[pl.ds(halo + r0, ch), c0:c0 + LANES] * conv).astype(za_ref.dtype)
            g_s[pl.ds(r0, ch), :] = _gelu_tanh(y_s[pl.ds(halo + r0, ch), c0 + 3 * LANES:c0 + 4 * LANES])
            xb = y_s[pl.ds(r0, ext), c0 + 4 * LANES:c0 + 5 * LANES]
            u = (cb[0] * jnp.where(first | second, 0.0, shifted(xb, 2))
                 + cb[1] * jnp.where(first, 0.0, shifted(xb, 1))
                 + cb[2] * xb[halo:halo + ch]
                 + cb[3] * jnp.where(last, 0.0, shifted(xb, -1))
                 + cb_bias)
            gates = _dot(u.astype(BF16), gw_ref[0, q])
            for k, ((ba, bi, cl), a_s, b_s) in enumerate(zip(dirs, (af_s, ab_s), (bf_s, bb_s))):
                r = _sigmoid(gates[:, 2 * k * LANES:(2 * k + 1) * LANES] + ba)
                ig = _sigmoid(gates[:, (2 * k + 1) * LANES:(2 * k + 2) * LANES] + bi)
                a = jnp.exp(cl * r)
                a_s[pl.ds(r0, ch), :] = a
                b_s[pl.ds(r0, ch), :] = jnp.sqrt(1.0 - a * a) * (ig * u)

    per = mm_rows // ch

    def warmup(c, carry):
        mm(c)
        return carry

    lax.fori_loop(0, 2, warmup, 0)
    stage_a(0)

    def pipelined(k, carry):
        stage_a(per * k - 3)
        stage_a(per * k - 2)
        mm(k)
        return carry

    lax.fori_loop(2, n_mm, pipelined, 0)

    def tail(e, carry):
        stage_a(e)
        return carry

    lax.fori_loop(n_sub - 3, n_sub, tail, 0)

    n_ctx_chunk = n_ctx // ch

    def scans(k, carry):
        c = jnp.where(k < n_ctx_chunk, n_ctx_chunk - 1 - k, n_sub - 1 - (k - n_ctx_chunk))
        out = []
        for q in range(hp):
            _, af_s, bf_s, ab_s, bb_s = sets[q]
            hf, hb = carry[q]
            hf = _chunk_scan(af_s, bf_s, pl.multiple_of(k * ch, ch), hf, row8, False)
            hb = _chunk_scan(ab_s, bb_s, pl.multiple_of(c * ch, ch), hb, row8, True)
            out.append((hf, hb))
        return tuple(out)

    zero_state = jnp.zeros((SUBLANES, LANES), F32)
    lax.fori_loop(0, n_sub, scans, tuple((zero_state, zero_state) for _ in range(hp)))

    def combine(e, carry):
        r0 = pl.multiple_of(e * ch, ch)
        for q in range(hp):
            g_s, _, bf_s, _, bb_s = sets[q]
            zb_ref[0, pl.ds(r0, ch), q * LANES:(q + 1) * LANES] = (
                g_s[pl.ds(r0, ch), :] * (bf_s[pl.ds(r0, ch), :] + bb_s[pl.ds(r0, ch), :])).astype(zb_ref.dtype)
        return carry

    lax.fori_loop(0, n_sub, combine, 0)


def _mixer0(h, w_in, conv_a_w, conv_b_w, conv_b_b, lru_wa, lru_ba, lru_wi, lru_bi, lru_lam, n_ctx):
    bsz, t_all, d = h.shape
    width = conv_a_w.shape[-1]
    heads = width // LANES
    assert lru_wa.shape[1] == heads and lru_wa.shape[2] == LANES
    mm_rows = 2 * SCAN_CHUNK
    assert t_all % mm_rows == 0 and t_all // mm_rows >= 2
    w_r = w_in.reshape(d, 5, heads, LANES).transpose(2, 0, 1, 3).reshape(heads, d, 5 * LANES).astype(BF16)
    gw = jnp.concatenate([lru_wa[0], lru_wi[0], lru_wa[1], lru_wi[1]], axis=-1).astype(BF16)

    def per_head(v):
        return v.reshape(-1, heads, LANES).transpose(1, 0, 2)

    prm = jnp.concatenate([
        per_head(conv_a_w), per_head(conv_b_w), per_head(conv_b_b[None]),
        per_head(lru_ba[0:1]), per_head(lru_bi[0:1]), per_head(lru_lam[0:1]),
        per_head(lru_ba[1:2]), per_head(lru_bi[1:2]), per_head(lru_lam[1:2]),
        jnp.zeros((heads, 2, LANES), F32)], axis=1)
    hp = MIXER_HEADS_PER_STEP
    assert heads % hp == 0
    w_r = w_r.reshape(heads // hp, hp, d, 5 * LANES).transpose(0, 2, 1, 3).reshape(heads // hp, d, hp * 5 * LANES)
    gw = gw.reshape(heads // hp, hp, LANES, 4 * LANES)
    prm = prm.reshape(heads // hp, hp, 16, LANES)
    out_sds = jax.ShapeDtypeStruct((bsz, t_all, width), BF16)
    out_spec = pl.BlockSpec((1, t_all, hp * LANES), lambda b, j: (b, 0, j))
    seq_scratch = pltpu.VMEM((t_all, LANES), F32)
    return pl.pallas_call(
        functools.partial(_mixer_kernel, t_all=t_all, n_ctx=n_ctx, mm_rows=mm_rows, hp=hp),
        grid=(bsz, heads // hp),
        in_specs=[pl.BlockSpec((1, t_all, d), lambda b, j: (b, 0, 0), pipeline_mode=pl.Buffered(1)),
                  pl.BlockSpec((1, d, hp * 5 * LANES), lambda b, j: (j, 0, 0)),
                  pl.BlockSpec((1, hp, LANES, 4 * LANES), lambda b, j: (j, 0, 0, 0)),
                  pl.BlockSpec((1, hp, 16, LANES), lambda b, j: (j, 0, 0, 0))],
        out_specs=[out_spec, out_spec],
        out_shape=[out_sds, out_sds],
        scratch_shapes=[pltpu.VMEM((t_all + 2 * SUBLANES, hp * 5 * LANES), F32)] + [seq_scratch] * (5 * hp),
        compiler_params=_cparams("arbitrary", "arbitrary"),
        name="mixer0_conv_lru",
    )(h, w_r, gw, prm)


def _split_router_weights(rw_ref, rw2_ref):
    n_exp = N_GROUPS * EXPERTS_PER_GROUP
    rw = rw_ref[...]
    rw_hi = rw.astype(BF16).astype(F32)
    rw2_ref[...] = (rw_hi + pltpu.roll(rw - rw_hi, n_exp, 1)).astype(rw2_ref.dtype)


def _route(f, rw2_ref, rb_ref, carry_ref, info_ref, meta_ref, cnt_ref, tm):
    n_exp = N_GROUPS * EXPERTS_PER_GROUP
    f_hi = f.astype(BF16)
    f_lo = (f - f_hi.astype(F32)).astype(BF16)
    rw2 = rw2_ref[...]
    lt = (_dot(f_hi, rw2) + _dot(f_lo, rw2)).T
    scores = _sigmoid(lt[0:n_exp] + lt[n_exp:2 * n_exp])
    sel = scores + rb_ref[...]
    s_rows = [scores[e:e + 1] for e in range(n_exp)]
    v_rows = [sel[e:e + 1] for e in range(n_exp)]
    best = None
    gidx = None
    for g in range(N_GROUPS):
        a, b, c, d = v_rows[4 * g:4 * g + 4]
        hi1, lo1 = jnp.maximum(a, b), jnp.minimum(a, b)
        hi2, lo2 = jnp.maximum(c, d), jnp.minimum(c, d)
        gs = jnp.maximum(hi1, hi2) + jnp.maximum(jnp.minimum(hi1, hi2), jnp.maximum(lo1, lo2))
        if g == 0:
            best, gidx = gs, jnp.zeros(gs.shape, jnp.int32)
        else:
            better = gs > best
            gidx = jnp.where(better, g, gidx)
            best = jnp.where(better, gs, best)
    found = None
    for e in range(n_exp):
        g = e // EXPERTS_PER_GROUP
        rank = jnp.zeros(best.shape, jnp.int32)
        for e2 in range(4 * g, 4 * g + 4):
            if e2 == e:
                continue
            ahead = v_rows[e2] > v_rows[e]
            if e2 < e:
                ahead = ahead | (v_rows[e2] == v_rows[e])
            rank = rank + ahead.astype(jnp.int32)
        chosen = (gidx == g) & (rank < 2)
        if e == 0:
            found = chosen
            lo_idx = jnp.zeros(best.shape, jnp.int32)
            hi_idx = jnp.zeros(best.shape, jnp.int32)
            lo_s = s_rows[0]
            hi_s = s_rows[0]
        else:
            take_lo = chosen & jnp.logical_not(found)
            lo_idx = jnp.where(take_lo, e, lo_idx)
            lo_s = jnp.where(take_lo, s_rows[e], lo_s)
            hi_idx = jnp.where(chosen, e, hi_idx)
            hi_s = jnp.where(chosen, s_rows[e], hi_s)
            found = found | chosen
    denom = lo_s + hi_s
    gate_lo = lo_s / denom
    gate_hi = hi_s / denom
    li = lo_idx - 4 * gidx
    hj = hi_idx - 4 * gidx
    pair = jnp.where(li == 0, 0, jnp.where(li == 1, 3, 5)) + hj - li - 1
    cls = PAIRS_PER_GROUP * gidx + pair
    n_cls_pad = 32
    onehot = (lax.broadcasted_iota(jnp.int32, (n_cls_pad, tm), 0) == cls).astype(F32)
    tri = (lax.broadcasted_iota(jnp.int32, (tm, tm), 0)
           < lax.broadcasted_iota(jnp.int32, (tm, tm), 1)).astype(BF16)
    excl = _dot(onehot.astype(BF16), tri)
    rank_in_cls = jnp.sum(onehot * (excl + carry_ref[:, 0:1]), axis=0, keepdims=True)
    carry_ref[...] = carry_ref[...] + jnp.sum(onehot, axis=1, keepdims=True)
    cnt_ref[...] = carry_ref[...]
    cls_f = cls.astype(F32)
    rid = lax.broadcasted_iota(jnp.int32, (LANES, tm), 0)
    packed = jnp.where(rid == 1, gate_lo, jnp.where(rid == 2, gate_hi, 0.0))
    info_ref[...] = packed.T
    rid8 = lax.broadcasted_iota(jnp.int32, (SUBLANES, tm), 0)
    meta_ref[0] = jnp.where(rid8 == 0, cls_f, jnp.where(rid8 == 1, rank_in_cls, 0.0))


def _post_mixer(xin_cols, y_cols, mod_ref, g_ref, rw_ref, rb_ref, outs, scratch, tm):
    x1_ref, f_ref, info_ref, meta_ref, cnt_ref = outs
    carry_ref, rw2_ref = scratch
    first = (pl.program_id(0) == 0) & (pl.program_id(1) == 0)

    @pl.when(first)
    def _():
        carry_ref[...] = jnp.zeros(carry_ref.shape, F32)
        _split_router_weights(rw_ref, rw2_ref)

    mod = mod_ref[0, 0]
    d = mod.shape[-1]
    group = min(PROJ_COL_GROUP, d)
    ssq = None
    for g in range(d // group):
        cols = slice(g * group, (g + 1) * group)
        x1g = xin_cols(cols) + mod[2:3, cols] * y_cols(cols)
        x1_ref[0, :, cols] = x1g
        part = jnp.sum(x1g * x1g, axis=-1, keepdims=True)
        ssq = part if ssq is None else ssq + part
    f = (x1_ref[0] * lax.rsqrt(ssq * (1.0 / d) + EPS) * g_ref[...]) * (1.0 + mod[4:5]) + mod[3:4]
    f_ref[0] = f
    _route(f, rw2_ref, rb_ref, carry_ref, info_ref, meta_ref, cnt_ref, tm)


def _outproj0_kernel(za_ref, zb_ref, wa_ref, wb_ref, x_ref, ctx_ref, mod_ref, g_ref, rw_ref, rb_ref,
                     x1_ref, f_ref, info_ref, meta_ref, cnt_ref, carry_ref, rw2_ref, *, n_ctx_tiles, tm):
    is_ctx = pl.program_id(1) < n_ctx_tiles
    za, zb = za_ref[0], zb_ref[0]
    _post_mixer(lambda cols: jnp.where(is_ctx, ctx_ref[0, :, cols], x_ref[0, :, cols]),
                lambda cols: _dot(za, wa_ref[:, cols]) + _dot(zb, wb_ref[:, cols]),
                mod_ref, g_ref, rw_ref, rb_ref, (x1_ref, f_ref, info_ref, meta_ref, cnt_ref),
                (carry_ref, rw2_ref), tm)


def _outproj1_kernel(z_ref, w_ref, x_ref, mod_ref, g_ref, rw_ref, rb_ref,
                     x1_ref, f_ref, info_ref, meta_ref, cnt_ref, carry_ref, rw2_ref, *, tm):
    z = z_ref[0]
    _post_mixer(lambda cols: x_ref[0, :, cols], lambda cols: _dot(z, w_ref[:, cols]),
                mod_ref, g_ref, rw_ref, rb_ref, (x1_ref, f_ref, info_ref, meta_ref, cnt_ref),
                (carry_ref, rw2_ref), tm)


def _router_operands(router_w, router_b):
    d, n_exp = router_w.shape
    rw = jnp.zeros((d, LANES), F32).at[:, :n_exp].set(router_w)
    return rw, router_b.reshape(n_exp, 1)


def _post_outs(bsz, tiles, tm, d):
    n_tok = bsz * tiles * tm
    out_specs = [pl.BlockSpec((1, tm, d), lambda b, i: (b, i, 0)),
                 pl.BlockSpec((1, tm, d), lambda b, i: (b, i, 0)),
                 pl.BlockSpec((tm, LANES), lambda b, i: (b * tiles + i, 0)),
                 pl.BlockSpec((1, SUBLANES, tm), lambda b, i: (b * tiles + i, 0, 0)),
                 pl.BlockSpec((32, LANES), lambda b, i: (0, 0))]
    out_shape = [jax.ShapeDtypeStruct((bsz, tiles * tm, d), F32),
                 jax.ShapeDtypeStruct((bsz, tiles * tm, d), F32),
                 jax.ShapeDtypeStruct((n_tok, LANES), F32),
                 jax.ShapeDtypeStruct((bsz * tiles, SUBLANES, tm), F32),
                 jax.ShapeDtypeStruct((32, LANES), F32)]
    return out_specs, out_shape


def _outproj0(za, zb, w_out, x, ctx, mod, g_ffn, router_w, router_b, layer):
    bsz, t_all, width = za.shape
    d = x.shape[-1]
    tm = TOKEN_TILE
    nct = ctx.shape[1] // tm
    tiles = t_all // tm
    w_bf = w_out.astype(BF16)
    rw, rb = _router_operands(router_w, router_b)
    x_spec, c_spec = _dual_specs(bsz, tm, d, nct)
    z_spec = pl.BlockSpec((1, tm, width), lambda b, i: (b, i, 0))
    out_specs, out_shape = _post_outs(bsz, tiles, tm, d)
    return pl.pallas_call(
        functools.partial(_outproj0_kernel, n_ctx_tiles=nct, tm=tm),
        grid=(bsz, tiles),
        in_specs=[z_spec, z_spec,
                  pl.BlockSpec((width, d), lambda b, i: (0, 0)),
                  pl.BlockSpec((width, d), lambda b, i: (1, 0)),
                  x_spec, c_spec, _mod_spec(layer, bsz, nct, d),
                  pl.BlockSpec((1, d), lambda b, i: (0, 0)),
                  pl.BlockSpec((d, LANES), lambda b, i: (0, 0)),
                  pl.BlockSpec(rb.shape, lambda b, i: (0, 0))],
        out_specs=out_specs,
        out_shape=out_shape,
        scratch_shapes=[pltpu.VMEM((32, LANES), F32), pltpu.VMEM((d, LANES), BF16)],
        compiler_params=_cparams("arbitrary", "arbitrary"),
        name="outproj0_router",
    )(za, zb, w_bf, w_bf, x, ctx, mod, g_ffn[layer:layer + 1], rw, rb)


def _outproj1(z, w_out, x_all, mod, g_ffn, router_w, router_b, layer, n_ctx):
    bsz, n_lat, width = z.shape
    d = x_all.shape[-1]
    tm = TOKEN_TILE
    nct = n_ctx // tm
    tiles = n_lat // tm
    rw, rb = _router_operands(router_w, router_b)
    out_specs, out_shape = _post_outs(bsz, tiles, tm, d)
    return pl.pallas_call(
        functools.partial(_outproj1_kernel, tm=tm),
        grid=(bsz, tiles),
        in_specs=[pl.BlockSpec((1, tm, width), lambda b, i: (b, i, 0)),
                  pl.BlockSpec((width, d), lambda b, i: (0, 0)),
                  pl.BlockSpec((1, tm, d), lambda b, i: (b, i + nct, 0)),
                  pl.BlockSpec((1, 1, 6, d), lambda b, i: (layer, b, 0, 0)),
                  pl.BlockSpec((1, d), lambda b, i: (0, 0)),
                  pl.BlockSpec((d, LANES), lambda b, i: (0, 0)),
                  pl.BlockSpec(rb.shape, lambda b, i: (0, 0))],
        out_specs=out_specs,
        out_shape=out_shape,
        scratch_shapes=[pltpu.VMEM((32, LANES), F32), pltpu.VMEM((d, LANES), BF16)],
        compiler_params=_cparams("arbitrary", "arbitrary"),
        name="outproj1_router",
    )(z, w_out.astype(BF16), x_all, mod, g_ffn[layer:layer + 1], rw, rb)


def _dispatch_plan(meta, counts, n_tok, tme):
    cls = meta[:, 0, :].reshape(-1).astype(jnp.int32)
    rank = meta[:, 1, :].reshape(-1).astype(jnp.int32)
    cnt = counts[:N_CLASSES, 0].astype(jnp.int32)
    padded = ((cnt + tme - 1) // tme) * tme
    ends = jnp.cumsum(padded)
    starts = ends - padded
    pos = starts[cls] + rank
    r_pad = n_tok + N_CLASSES * tme
    n_tiles = r_pad // tme
    cls_tiles = padded // tme
    cls_tile_start = starts // tme
    used_tiles = ends[-1] // tme
    spare = used_tiles + jnp.arange(N_CLASSES, dtype=jnp.int32)
    fill_tile = jnp.concatenate([ends // tme - 1, spare])
    fill_ok = jnp.concatenate([cnt > 0, spare < n_tiles])
    fill = (jnp.where(fill_ok, fill_tile, 0).astype(jnp.int32), fill_ok.astype(jnp.int32))
    seg_cls = np.array([[c for c in range(N_CLASSES) if _CLASS_LO[c] == e or _CLASS_HI[c] == e]
                        for e in range(N_GROUPS * EXPERTS_PER_GROUP)], np.int32).reshape(-1)
    seg_exp = np.repeat(np.arange(N_GROUPS * EXPERTS_PER_GROUP, dtype=np.int32), 3)
    seg_slot = (_CLASS_HI[seg_cls] == seg_exp).astype(np.int32)
    seg_len = cls_tiles[seg_cls]
    seg_end = jnp.cumsum(seg_len)
    seg_start = seg_end - seg_len
    total = seg_end[-1]
    n_steps = 2 * n_tiles
    q = jnp.arange(n_steps, dtype=jnp.int32)
    qc = jnp.minimum(q, total - 1)
    seg = jnp.sum((seg_end[None, :] <= qc[:, None]).astype(jnp.int32), axis=1)
    real = q < total
    extra = q - total
    step_tile = jnp.where(real, cls_tile_start[seg_cls][seg] + (qc - seg_start[seg]), total // 2 + extra // 2)
    step_exp = jnp.asarray(seg_exp)[seg]
    step_slot = jnp.where(real, jnp.asarray(seg_slot)[seg], extra % 2)
    step_valid = real.astype(jnp.int32)
    n_exp = N_GROUPS * EXPERTS_PER_GROUP
    has_tiles = jnp.sum(seg_len.reshape(n_exp, 3), axis=1) > 0
    ids = jnp.arange(n_exp, dtype=jnp.int32)
    later = jnp.where(has_tiles[None, :] & (ids[None, :] > ids[:, None]), ids[None, :], n_exp)
    next_exp = jnp.min(later, axis=1)
    next_exp = jnp.where(next_exp == n_exp, -1, next_exp).astype(jnp.int32)
    prev_exp = jnp.concatenate([jnp.full((1,), -1, jnp.int32), step_exp[:-1]])
    step_first = (real & (step_exp != prev_exp)).astype(jnp.int32)
    step_next = next_exp[step_exp]
    return pos, fill, r_pad, (step_exp, step_tile, step_slot, step_valid, step_first, step_next)


def _gather_rows(idx_ref, nxt_ref, src_hbm, buf, sem, rows, step, n_steps):
    def issue(ref, slot):
        def body(g, carry):
            for u in range(DMA_ISSUE_UNROLL):
                r = g * DMA_ISSUE_UNROLL + u
                pltpu.make_async_copy(src_hbm.at[pl.ds(ref[0, 0, r], 1), :], buf.at[slot, pl.ds(r, 1), :],
                                      sem.at[slot]).start(priority=u % 2)
            return carry
        lax.fori_loop(0, rows // DMA_ISSUE_UNROLL, body, 0)

    slot = step % 2

    @pl.when(step == 0)
    def _():
        issue(idx_ref, 0)

    @pl.when(step + 1 < n_steps)
    def _():
        issue(nxt_ref, 1 - slot)

    pltpu.make_async_copy(src_hbm.at[pl.ds(0, rows), :], buf.at[slot], sem.at[slot]).wait()
    return slot


def _scatter_sorted_kernel(fill_tile_ref, fill_ok_ref, pos_ref, f_ref, xs_hbm, zero_s, sem_z, sem,
                           *, rows, tme, n_fill):
    @pl.when(pl.program_id(0) == 0)
    def _():
        zero_s[...] = jnp.zeros(zero_s.shape, zero_s.dtype)

        def fill(k):
            r0 = pl.multiple_of(fill_tile_ref[k] * tme, tme)
            return pltpu.make_async_copy(zero_s, xs_hbm.at[pl.ds(r0, tme), :], sem_z)

        def start(k, carry):
            @pl.when(fill_ok_ref[k] == 1)
            def _():
                fill(k).start()
            return carry

        def wait(k, carry):
            @pl.when(fill_ok_ref[k] == 1)
            def _():
                fill(k).wait()
            return carry

        lax.fori_loop(0, n_fill, start, 0)
        lax.fori_loop(0, n_fill, wait, 0)

    def body(g, carry):
        for u in range(DMA_ISSUE_UNROLL):
            r = g * DMA_ISSUE_UNROLL + u
            pltpu.make_async_copy(f_ref.at[pl.ds(r, 1), :], xs_hbm.at[pl.ds(pos_ref[0, 0, r], 1), :],
                                  sem).start(priority=u % 2)
        return carry

    lax.fori_loop(0, rows // DMA_ISSUE_UNROLL, body, 0)
    pltpu.make_async_copy(f_ref, xs_hbm.at[pl.ds(0, rows), :], sem).wait()


def _scatter_sorted(f_flat, pos, fill, r_pad, tme):
    n_tok, d = f_flat.shape
    tm = TOKEN_TILE
    n_steps = n_tok // tm
    fill_tile, fill_ok = fill
    grid_spec = pltpu.PrefetchScalarGridSpec(
        num_scalar_prefetch=2,
        grid=(n_steps,),
        in_specs=[pl.BlockSpec((1, 1, tm), lambda i, ft, fo: (i, 0, 0), memory_space=pltpu.SMEM),
                  pl.BlockSpec((tm, d), lambda i, ft, fo: (i, 0))],
        out_specs=pl.BlockSpec(memory_space=pl.ANY),
        scratch_shapes=[pltpu.VMEM((tme, d), F32), pltpu.SemaphoreType.DMA, pltpu.SemaphoreType.DMA],
    )
    return pl.pallas_call(
        functools.partial(_scatter_sorted_kernel, rows=tm, tme=tme, n_fill=fill_tile.shape[0]),
        grid_spec=grid_spec,
        out_shape=jax.ShapeDtypeStruct((r_pad, d), F32),
        compiler_params=_cparams("arbitrary"),
        name="moe_scatter_sorted",
    )(fill_tile, fill_ok, pos.reshape(n_steps, 1, tm), f_flat)


def _expert_kernel(exp_ref, tile_ref, slot_ref, valid_ref, first_ref, next_ref, x_ref, wg_hbm, wu_hbm, wd_hbm,
                   o_ref, wg_s, wu_s, wd_s, stage_g, stage_u, stage_d, sem, *, layer):
    s = pl.program_id(0)

    def weight_copies(e):
        return (pltpu.make_async_copy(wg_hbm.at[layer, e], stage_g, sem.at[0]),
                pltpu.make_async_copy(wu_hbm.at[layer, e], stage_u, sem.at[1]),
                pltpu.make_async_copy(wd_hbm.at[layer, e], stage_d, sem.at[2]))

    def cast(stage, dst):
        def body(i, carry):
            r0 = pl.multiple_of(i * EXPERT_CAST_ROWS, EXPERT_CAST_ROWS)
            dst[pl.ds(r0, EXPERT_CAST_ROWS), :] = stage[pl.ds(r0, EXPERT_CAST_ROWS), :].astype(dst.dtype)
            return carry
        lax.fori_loop(0, stage.shape[0] // EXPERT_CAST_ROWS, body, 0)

    @pl.when(first_ref[s] == 1)
    def _():
        @pl.when(s == 0)
        def _():
            for cp in weight_copies(exp_ref[s]):
                cp.start()

        for cp in weight_copies(exp_ref[s]):
            cp.wait()
        cast(stage_g, wg_s)
        cast(stage_u, wu_s)
        cast(stage_d, wd_s)

        @pl.when(next_ref[s] >= 0)
        def _():
            for cp in weight_copies(next_ref[s]):
                cp.start()

    @pl.when(valid_ref[s] == 1)
    def _():
        xv = x_ref[...].astype(BF16)
        hg = _dot(xv, wg_s[...])
        hu = _dot(xv, wu_s[...])
        hid = (hg * _sigmoid(hg) * hu).astype(BF16)
        o_ref[...] = _dot(hid, wd_s[...])

    @pl.when(valid_ref[s] == 0)
    def _():
        o_ref[...] = jnp.zeros(o_ref.shape, o_ref.dtype)


def _experts(xs, plan, w_gate, w_up, w_down, layer, tme):
    r_pad, d = xs.shape
    ff = w_gate.shape[-1]
    assert d % EXPERT_CAST_ROWS == 0 and ff % EXPERT_CAST_ROWS == 0
    n_steps = plan[0].shape[0]
    any_spec = pl.BlockSpec(memory_space=pl.ANY)
    grid_spec = pltpu.PrefetchScalarGridSpec(
        num_scalar_prefetch=6,
        grid=(n_steps,),
        in_specs=[pl.BlockSpec((tme, d), lambda s, e, t, w, v, f, n: (t[s], 0)), any_spec, any_spec, any_spec],
        out_specs=pl.BlockSpec((tme, d), lambda s, e, t, w, v, f, n: (t[s], w[s])),
        scratch_shapes=[pltpu.VMEM((d, ff), BF16), pltpu.VMEM((d, ff), BF16), pltpu.VMEM((ff, d), BF16),
                        pltpu.VMEM((d, ff), F32), pltpu.VMEM((d, ff), F32), pltpu.VMEM((ff, d), F32),
                        pltpu.SemaphoreType.DMA((3,))],
    )
    return pl.pallas_call(
        functools.partial(_expert_kernel, layer=layer),
        grid_spec=grid_spec,
        out_shape=jax.ShapeDtypeStruct((r_pad, 2 * d), F32),
        compiler_params=_cparams("arbitrary"),
        name="moe_experts",
    )(*plan, xs, w_gate, w_up, w_down)


def _moe_residual(rows, info_ref, x1_ref, mod_ref, d):
    info = info_ref[...]
    y = info[:, 1:2] * rows[:, 0:d] + info[:, 2:3] * rows[:, d:2 * d]
    return x1_ref[0] + mod_ref[0, 0][5:6] * y


def _linear_step(tiles):
    return pl.program_id(0) * tiles + pl.program_id(1)


def _combine0_kernel(idx_ref, nxt_ref, o_hbm, info_ref, x1_ref, mod_ref, modn_ref, g_ref, x2_ref, h_ref,
                     buf, sem, *, rows, d, tiles, n_steps):
    slot = _gather_rows(idx_ref, nxt_ref, o_hbm, buf, sem, rows, _linear_step(tiles), n_steps)
    x2 = _moe_residual(buf[slot], info_ref, x1_ref, mod_ref, d)
    x2_ref[0] = x2
    modn = modn_ref[0, 0]
    h_ref[0] = _mod_norm(x2, g_ref[...], modn[1:2], modn[0:1]).astype(h_ref.dtype)


def _combine1_kernel(idx_ref, nxt_ref, o_hbm, info_ref, x1_ref, mod_ref, g_ref, out_ref, buf, sem,
                     *, rows, d, tiles, n_steps):
    slot = _gather_rows(idx_ref, nxt_ref, o_hbm, buf, sem, rows, _linear_step(tiles), n_steps)
    x2 = _moe_residual(buf[slot], info_ref, x1_ref, mod_ref, d)
    ms = jnp.mean(x2 * x2, axis=-1, keepdims=True)
    out_ref[0] = x2 * lax.rsqrt(ms + EPS) * g_ref[...]


def _pos_specs(tiles, n_steps, tm):
    cur = pl.BlockSpec((1, 1, tm), lambda b, i: (b * tiles + i, 0, 0), memory_space=pltpu.SMEM)
    nxt = pl.BlockSpec((1, 1, tm), lambda b, i: (jnp.minimum(b * tiles + i + 1, n_steps - 1), 0, 0),
                       memory_space=pltpu.SMEM)
    return cur, nxt


def _combine0(o_sorted, pos, info, x1, mod, norm_mix_g, layer, n_ctx):
    bsz, t_all, d = x1.shape
    tm = TOKEN_TILE
    tiles = t_all // tm
    nct = n_ctx // tm
    n_steps = bsz * tiles
    tok_spec = pl.BlockSpec((1, tm, d), lambda b, i: (b, i, 0))
    cur, nxt = _pos_specs(tiles, n_steps, tm)
    idx = pos.reshape(n_steps, 1, tm)
    return pl.pallas_call(
        functools.partial(_combine0_kernel, rows=tm, d=d, tiles=tiles, n_steps=n_steps),
        grid=(bsz, tiles),
        in_specs=[cur, nxt,
                  pl.BlockSpec(memory_space=pl.ANY),
                  pl.BlockSpec((tm, LANES), lambda b, i: (b * tiles + i, 0)),
                  tok_spec,
                  _mod_spec(layer, bsz, nct, d),
                  _mod_spec(layer + 1, bsz, nct, d),
                  pl.BlockSpec((1, d), lambda b, i: (0, 0))],
        out_specs=[tok_spec, tok_spec],
        out_shape=[jax.ShapeDtypeStruct((bsz, t_all, d), F32), jax.ShapeDtypeStruct((bsz, t_all, d), BF16)],
        scratch_shapes=[pltpu.VMEM((2, tm, 2 * d), F32), pltpu.SemaphoreType.DMA((2,))],
        compiler_params=_cparams("arbitrary", "arbitrary"),
        name="moe_combine0",
    )(idx, idx, o_sorted, info, x1, mod, mod, norm_mix_g[layer + 1:layer + 2])


def _combine1(o_sorted, pos, info, x1, mod, final_g, layer):
    bsz, n_lat, d = x1.shape
    tm = TOKEN_TILE
    tiles = n_lat // tm
    n_steps = bsz * tiles
    tok_spec = pl.BlockSpec((1, tm, d), lambda b, i: (b, i, 0))
    cur, nxt = _pos_specs(tiles, n_steps, tm)
    idx = pos.reshape(n_steps, 1, tm)
    return pl.pallas_call(
        functools.partial(_combine1_kernel, rows=tm, d=d, tiles=tiles, n_steps=n_steps),
        grid=(bsz, tiles),
        in_specs=[cur, nxt,
                  pl.BlockSpec(memory_space=pl.ANY),
                  pl.BlockSpec((tm, LANES), lambda b, i: (b * tiles + i, 0)),
                  tok_spec,
                  pl.BlockSpec((1, 1, 6, d), lambda b, i: (layer, b, 0, 0)),
                  pl.BlockSpec((1, d), lambda b, i: (0, 0))],
        out_specs=tok_spec,
        out_shape=jax.ShapeDtypeStruct((bsz, n_lat, d), F32),
        scratch_shapes=[pltpu.VMEM((2, tm, 2 * d), F32), pltpu.SemaphoreType.DMA((2,))],
        compiler_params=_cparams("arbitrary", "arbitrary"),
        name="moe_combine1",
    )(idx, idx, o_sorted, info, x1, mod, final_g.reshape(1, d))


def _moe_sorted_outputs(f, meta, counts, expert_w, layer):
    d = f.shape[-1]
    f_flat = f.reshape(-1, d)
    n_tok = f_flat.shape[0]
    tme = EXPERT_TILE
    pos, fill, r_pad, plan = _dispatch_plan(meta, counts, n_tok, tme)
    xs = _scatter_sorted(f_flat, pos, fill, r_pad, tme)
    return _experts(xs, plan, *expert_w, layer, tme), pos


def _rope_tables(n_ctx, n_lat, qk_dim):
    axis_dim = qk_dim // 2
    half = axis_dim // 2
    rows = n_lat // GRID_W
    row = jnp.repeat(jnp.arange(rows, dtype=F32), GRID_W)
    col = jnp.tile(jnp.arange(GRID_W, dtype=F32), rows)
    inv = ROPE_BASE ** (-jnp.arange(0, axis_dim, 2, dtype=F32) / axis_dim)
    ang_r = row[:, None] * inv
    ang_c = col[:, None] * inv
    ang = jnp.concatenate([ang_r, ang_r, ang_c, ang_c], axis=-1)
    cos, sin = jnp.cos(ang), jnp.sin(ang)
    lower = (jnp.arange(qk_dim) % axis_dim) < half
    sin_up = jnp.where(lower, -sin, 0.0)
    sin_dn = jnp.where(lower, 0.0, sin)
    pad = jnp.zeros((n_ctx, qk_dim), F32)
    return (jnp.concatenate([pad + 1.0, cos], axis=0), jnp.concatenate([pad, sin_up], axis=0),
            jnp.concatenate([pad, sin_dn], axis=0))


def _proj_kernel(*refs, qk_dim, mult, rope, group):
    if rope:
        h_ref, w_ref, cos_ref, su_ref, sd_ref, o_ref = refs
        cos, su, sd = cos_ref[...], su_ref[...], sd_ref[...]
        half = qk_dim // 4
    else:
        h_ref, w_ref, o_ref = refs
    h = h_ref[0]
    for g in range(w_ref.shape[1] // group):
        y = _dot(h, w_ref[:, g * group:(g + 1) * group])
        if not rope:
            o_ref[0, :, g * group:(g + 1) * group] = y.astype(o_ref.dtype)
            continue
        for k in range(group // qk_dim):
            t = y[:, k * qk_dim:(k + 1) * qk_dim]
            up = pltpu.roll(t, qk_dim - half, 1)
            dn = pltpu.roll(t, half, 1)
            r = t * cos + up * su + dn * sd
            if mult != 1.0:
                r = r * mult
            c0 = g * group + k * qk_dim
            o_ref[0, :, c0:c0 + qk_dim] = r.astype(o_ref.dtype)


def _projection(h, w_bf, col_block, row_off, n_rows, tables, mult, qk_dim, name):
    bsz, _, d = h.shape
    cols = w_bf.shape[1] // 3
    tm = _pick_tile(math.gcd(n_rows, row_off) if row_off else n_rows, PROJ_TILES)
    off = row_off // tm
    tab = pl.BlockSpec((tm, qk_dim), lambda b, i: (i, 0))
    in_specs = [pl.BlockSpec((1, tm, d), lambda b, i: (b, i + off, 0)),
                pl.BlockSpec((d, cols), lambda b, i: (0, col_block))]
    operands = [h, w_bf]
    if tables is not None:
        in_specs += [tab, tab, tab]
        operands += list(tables)
    return pl.pallas_call(
        functools.partial(_proj_kernel, qk_dim=qk_dim, mult=mult, rope=tables is not None,
                          group=min(PROJ_COL_GROUP, cols)),
        grid=(bsz, n_rows // tm),
        in_specs=in_specs,
        out_specs=pl.BlockSpec((1, tm, cols), lambda b, i: (b, i, 0)),
        out_shape=jax.ShapeDtypeStruct((bsz, n_rows, cols), BF16),
        compiler_params=_cparams("arbitrary", "arbitrary"),
        name=name,
    )(*operands)


def _qkv(h, w_in_o, n_ctx, qk_dim):
    t_all = h.shape[1]
    n_lat = t_all - n_ctx
    w_bf = w_in_o.astype(BF16)
    cos, su, sd = _rope_tables(n_ctx, n_lat, qk_dim)
    q_scale = qk_dim ** -0.5 * math.log2(math.e)
    q = _projection(h, w_bf, 0, n_ctx, n_lat, (cos[n_ctx:], su[n_ctx:], sd[n_ctx:]), q_scale, qk_dim, "q_rope")
    k = _projection(h, w_bf, 1, 0, t_all, (cos, su, sd), 1.0, qk_dim, "k_rope")
    v = _projection(h, w_bf, 2, 0, t_all, None, 1.0, qk_dim, "v_proj")
    return q, k, v


def _attn_kernel(q_ref, k_ref, v_ref, lp_ref, g_ref, o_ref, s_a, e_a, l_a, s_b, e_b, l_b,
                 *, qk_dim, lam_init, tq, kc):
    lp = lp_ref[...]
    lam = (jnp.exp(jnp.sum(lp[0:1] * lp[1:2], axis=-1, keepdims=True))
           - jnp.exp(jnp.sum(lp[2:3] * lp[3:4], axis=-1, keepdims=True)) + lam_init)
    t_all = k_ref.shape[1]
    n_kc = t_all // kc
    n_q = q_ref.shape[1] // tq
    half = kc // 2
    nt = (((1,), (1,)), ((), ()))

    def row0(i):
        return i * tq if isinstance(i, int) else pl.multiple_of(i * tq, tq)

    def stage(i_soft, soft, i_val, val):
        pv = [None, None]
        for m in range(2):
            dims = slice(m * qk_dim, (m + 1) * qk_dim)
            if soft is not None:
                s_s, e_s, l_s = soft
                qm = q_ref[0, pl.ds(row0(i_soft), tq), dims]
            mx = None
            for c in range(n_kc):
                keys = slice(c * kc, (c + 1) * kc)
                if val is not None:
                    part = _dot(val[1][m, :, keys], v_ref[0, keys, :])
                    pv[m] = part if pv[m] is None else pv[m] + part
                if soft is not None:
                    s = lax.dot_general(qm, k_ref[0, keys, dims], nt,
                                        preferred_element_type=F32)
                    s_s[m, :, keys] = s
                    cm = jnp.maximum(s[:, :half], s[:, half:])
                    mx = cm if mx is None else jnp.maximum(mx, cm)
            if soft is not None:
                row_max = jnp.max(mx, axis=-1, keepdims=True)
                acc = None
                for c in range(n_kc):
                    keys = slice(c * kc, (c + 1) * kc)
                    e = jnp.exp2(s_s[m, :, keys] - row_max)
                    e_s[m, :, keys] = e.astype(e_s.dtype)
                    part = e[:, :half] + e[:, half:]
                    acc = part if acc is None else acc + part
                l_s[m] = 1.0 / jnp.sum(acc, axis=-1, keepdims=True)
        if val is not None:
            l_v = val[2]
            o = pv[0] * l_v[0] - lam * (pv[1] * l_v[1])
            o = o * lax.rsqrt(jnp.mean(o * o, axis=-1, keepdims=True) + EPS)
            o_ref[0, pl.ds(row0(i_val), tq), :] = (o * g_ref[...] * (1.0 - lam_init)).astype(o_ref.dtype)

    set_a, set_b = (s_a, e_a, l_a), (s_b, e_b, l_b)
    stage(0, set_a, None, None)

    def pair(j, carry):
        stage(2 * j + 1, set_b, 2 * j, set_a)
        stage(2 * j + 2, set_a, 2 * j + 1, set_b)
        return carry

    lax.fori_loop(0, n_q // 2 - 1, pair, 0)
    stage(n_q - 1, set_b, n_q - 2, set_a)
    stage(None, None, n_q - 1, set_b)


def _attention(q, k, v, lam_params, subln_g, qk_dim, lam_init):
    bsz, n_lat, cols = q.shape
    t_all = k.shape[1]
    v_dim = subln_g.shape[-1]
    heads = cols // v_dim
    tq, kc = ATTN_Q_TILE, ATTN_KEY_CHUNK
    assert n_lat % (2 * tq) == 0 and t_all % kc == 0
    buffer_set = [pltpu.VMEM((2, tq, t_all), F32), pltpu.VMEM((2, tq, t_all), BF16),
                  pltpu.VMEM((2, tq, 1), F32)]
    return pl.pallas_call(
        functools.partial(_attn_kernel, qk_dim=qk_dim, lam_init=lam_init, tq=tq, kc=kc),
        grid=(bsz, heads),
        in_specs=[pl.BlockSpec((1, n_lat, v_dim), lambda b, h: (b, 0, h)),
                  pl.BlockSpec((1, t_all, v_dim), lambda b, h: (b, 0, h)),
                  pl.BlockSpec((1, t_all, v_dim), lambda b, h: (b, 0, h)),
                  pl.BlockSpec((4, qk_dim), lambda b, h: (0, 0)),
                  pl.BlockSpec((1, v_dim), lambda b, h: (0, 0))],
        out_specs=pl.BlockSpec((1, n_lat, v_dim), lambda b, h: (b, 0, h)),
        out_shape=jax.ShapeDtypeStruct((bsz, n_lat, cols), BF16),
        scratch_shapes=buffer_set + buffer_set,
        compiler_params=_cparams("arbitrary", "arbitrary"),
        name="diff_attention",
    )(q, k, v, lam_params, subln_g)


def kernel(x, c, ctx, c_ctx, ada_w, ada_b, norm_mix_g, norm_ffn_g, final_g, w_in_e, conv_a_w, conv_b_w, conv_b_b, lru_wa, lru_ba, lru_wi, lru_bi, lru_lam, w_out_e, w_in_o, lam_q1, lam_k1, lam_q2, lam_k2, subln_g, w_out_o, router_w, router_b, exp_w_gate, exp_w_up, exp_w_down):
    bsz, n_lat, d = x.shape
    n_ctx = ctx.shape[1]
    depth = ada_w.shape[0]
    assert depth == 2 and w_in_e.shape[0] == 1 and w_in_o.shape[0] == 1
    assert n_ctx % TOKEN_TILE == 0 and n_lat % TOKEN_TILE == 0 and n_ctx % SCAN_CHUNK == 0
    assert router_w.shape[1] == N_GROUPS * EXPERTS_PER_GROUP
    qk_dim = lam_q1.shape[-1]
    assert qk_dim == LANES

    cond = jnp.concatenate([c, c_ctx[None]], axis=0)
    mod = _ada(cond, ada_w, ada_b).reshape(depth, bsz + 1, 6, d)

    h0 = _prenorm(x, ctx, mod, norm_mix_g, 0)
    za, zb = _mixer0(h0, w_in_e[0], conv_a_w[0], conv_b_w[0], conv_b_b[0], lru_wa[0], lru_ba[0],
                     lru_wi[0], lru_bi[0], lru_lam[0], n_ctx)
    x1, f0, info0, meta0, cnt0 = _outproj0(za, zb, w_out_e[0], x, ctx, mod, norm_ffn_g, router_w, router_b, 0)
    expert_w = (exp_w_gate, exp_w_up, exp_w_down)
    o0, pos0 = _moe_sorted_outputs(f0, meta0, cnt0, expert_w, 0)
    x2, h1 = _combine0(o0, pos0, info0, x1, mod, norm_mix_g, 0, n_ctx)

    lam_init = 0.8 - 0.6 * math.exp(-0.3 * 1)
    q, k, v = _qkv(h1, w_in_o[0], n_ctx, qk_dim)
    lam_params = jnp.concatenate([lam_q1, lam_k1, lam_q2, lam_k2], axis=0)
    att = _attention(q, k, v, lam_params, subln_g, qk_dim, lam_init)
    x3, f1, info1, meta1, cnt1 = _outproj1(att, w_out_o[0], x2, mod, norm_ffn_g, router_w, router_b, 1, n_ctx)
    o1, pos1 = _moe_sorted_outputs(f1, meta1, cnt1, expert_w, 1)
    return _combine1(o1, pos1, info1, x3, mod, final_g, 1)
```
